```python
import math
import jax, jax.numpy as jnp
from jax import lax
import numpy as np

D_MODEL = 2048
BATCH = 4
SEQ = 4096
DEPTH = 4

GMLP_WIDTH = D_MODEL // 2
GMLP_GROUPS = 8
GMLP_GROUP_DIM = GMLP_WIDTH // GMLP_GROUPS
CHUNK = 128

ATTN_HEADS = 8
ATTN_WIDTH = D_MODEL // 2
ATTN_V_DIM = ATTN_WIDTH // ATTN_HEADS
ATTN_QK_DIM = ATTN_V_DIM // 2
Q_BLOCK = 128
ROPE_THETA = 10000.0

N_BRANCH = 2
QK_WIDTH = ATTN_HEADS * 2 * ATTN_QK_DIM
SPLITS = (GMLP_WIDTH, 2 * GMLP_WIDTH, 2 * GMLP_WIDTH + QK_WIDTH, 2 * GMLP_WIDTH + 2 * QK_WIDTH,
          2 * GMLP_WIDTH + 2 * QK_WIDTH + ATTN_WIDTH)
D_IN = 2 * GMLP_WIDTH + 2 * QK_WIDTH + ATTN_WIDTH + N_BRANCH * D_MODEL

FFN_HIDDEN = -(-8 * D_MODEL // (3 * 256)) * 256
EPS = 1e-6

kernel_name = "hybrid_gmlp_diffattn_adaln_encoder"


def rms_norm(x, g):
    xf = x.astype(jnp.float32)
    y = xf * lax.rsqrt(jnp.mean(xf * xf, axis=-1, keepdims=True) + EPS)
    return (y * g.astype(jnp.float32)).astype(x.dtype)


def layer_norm(x, g, b):
    xf = x.astype(jnp.float32)
    mu = jnp.mean(xf, axis=-1, keepdims=True)
    var = jnp.mean(jnp.square(xf - mu), axis=-1, keepdims=True)
    y = (xf - mu) * lax.rsqrt(var + EPS)
    return (y * g.astype(jnp.float32) + b.astype(jnp.float32)).astype(x.dtype)


def rope_tables(seq_len, dim):
    pos = jnp.arange(seq_len, dtype=jnp.float32)
    inv_freq = ROPE_THETA ** (-jnp.arange(0, dim, 2, dtype=jnp.float32) / dim)
    ang = pos[:, None] * inv_freq[None, :]
    return jnp.cos(ang), jnp.sin(ang)


def apply_rope(x, cos, sin):
    cos = cos.astype(x.dtype)
    sin = sin.astype(x.dtype)
    x1, x2 = jnp.split(x, 2, axis=-1)
    return jnp.concatenate([x1 * cos - x2 * sin, x2 * cos + x1 * sin], axis=-1)


def gmlp_spatial_gate(u, v, ln_g, ln_b, w_s, b_s):
    bsz, seq = v.shape[0], v.shape[1]
    v = layer_norm(v, ln_g, ln_b)
    vr = v.reshape(bsz, seq // CHUNK, CHUNK, GMLP_GROUPS, GMLP_GROUP_DIM)
    mixed = jnp.einsum('gpq,bnqgd->bnpgd', w_s, vr) + b_s.T[None, None, :, :, None]
    return u * mixed.reshape(bsz, seq, GMLP_WIDTH)


def differential_attention(q, k, v, lam, sub_g, lambda_init):
    bsz, seq = q.shape[0], q.shape[1]
    cos, sin = rope_tables(seq, ATTN_QK_DIM)
    q = apply_rope(q.transpose(0, 2, 3, 1, 4), cos, sin)
    k = apply_rope(k.transpose(0, 2, 3, 1, 4), cos, sin)
    v = v.transpose(0, 2, 1, 3)
    scale = ATTN_QK_DIM ** -0.5
    n_blk = seq // Q_BLOCK
    qb = q.reshape(bsz, ATTN_HEADS, 2, n_blk, Q_BLOCK, ATTN_QK_DIM).transpose(3, 0, 1, 2, 4, 5)

    def block(q_blk):
        s = jnp.einsum('bhmqd,bhmkd->bhmqk', q_blk, k).astype(jnp.float32) * scale
        p = jax.nn.softmax(s, axis=-1)
        p_diff = p[:, :, 0] - lam * p[:, :, 1]
        return jnp.einsum('bhqk,bhkd->bhqd', p_diff.astype(v.dtype), v)

    o = lax.map(block, qb)
    o = o.transpose(1, 0, 3, 2, 4).reshape(bsz, seq, ATTN_HEADS, ATTN_V_DIM)
    o = rms_norm(o, sub_g) * (1.0 - lambda_init)
    return o.reshape(bsz, seq, ATTN_WIDTH)


def setup_inputs(seed: int = 0) -> dict:
    key = jax.random.key(seed)
    ks = jax.random.split(key, 23)
    f32 = jnp.float32
    L, D = DEPTH, D_MODEL

    def nrm(k, shape, scale):
        return jax.random.normal(k, shape, f32) * scale

    return {
        "x": nrm(ks[0], (BATCH, SEQ, D), 1.0),
        "c": nrm(ks[1], (BATCH, D), 1.0),
        "ada_w": nrm(ks[2], (L, D, 6 * D), 0.5 * D ** -0.5),
        "ada_b": nrm(ks[3], (L, 6 * D), 0.02),
        "norm1_g": 1.0 + nrm(ks[4], (L, D), 0.1),
        "norm2_g": 1.0 + nrm(ks[5], (L, D), 0.1),
        "w_in": nrm(ks[6], (L, D, D_IN), D ** -0.5),
        "gmlp_ln_g": 1.0 + nrm(ks[7], (L, GMLP_WIDTH), 0.1),
        "gmlp_ln_b": nrm(ks[8], (L, GMLP_WIDTH), 0.02),
        "w_s": nrm(ks[9], (L, GMLP_GROUPS, CHUNK, CHUNK), CHUNK ** -0.5),
        "b_s": 1.0 + nrm(ks[10], (L, GMLP_GROUPS, CHUNK), 0.1),
        "lambda_q1": nrm(ks[11], (L, ATTN_QK_DIM), 0.1),
        "lambda_k1": nrm(ks[12], (L, ATTN_QK_DIM), 0.1),
        "lambda_q2": nrm(ks[13], (L, ATTN_QK_DIM), 0.1),
        "lambda_k2": nrm(ks[14], (L, ATTN_QK_DIM), 0.1),
        "subln_g": 1.0 + nrm(ks[15], (L, ATTN_V_DIM), 0.1),
        "w_up_gmlp": nrm(ks[16], (L, GMLP_WIDTH, D), GMLP_WIDTH ** -0.5),
        "w_up_attn": nrm(ks[17], (L, ATTN_WIDTH, D), ATTN_WIDTH ** -0.5),
        "w_o": nrm(ks[18], (L, D, D), D ** -0.5),
        "ffn_w1": nrm(ks[19], (L, D, FFN_HIDDEN), D ** -0.5),
        "ffn_w3": nrm(ks[20], (L, D, FFN_HIDDEN), D ** -0.5),
        "ffn_w2": nrm(ks[21], (L, FFN_HIDDEN, D), FFN_HIDDEN ** -0.5),
        "final_g": 1.0 + nrm(ks[22], (D,), 0.1),
    }


def reference(x, c, ada_w, ada_b, norm1_g, norm2_g, w_in, gmlp_ln_g, gmlp_ln_b, w_s, b_s,
              lambda_q1, lambda_k1, lambda_q2, lambda_k2, subln_g, w_up_gmlp, w_up_attn, w_o,
              ffn_w1, ffn_w3, ffn_w2, final_g):
    bsz, seq = x.shape[0], x.shape[1]
    c_act = jax.nn.silu(c)
    for l in range(DEPTH):
        mod = jnp.dot(c_act, ada_w[l]) + ada_b[l]
        sh1, sc1, gt1, sh2, sc2, gt2 = [m[:, None, :] for m in jnp.split(mod, 6, axis=-1)]

        h = rms_norm(x, norm1_g[l]) * (1.0 + sc1) + sh1
        proj = jnp.dot(h, w_in[l])
        u, v, q, k, va, gate_logits = jnp.split(proj, SPLITS, axis=-1)

        branch_a = gmlp_spatial_gate(jax.nn.gelu(u), jax.nn.gelu(v),
                                     gmlp_ln_g[l], gmlp_ln_b[l], w_s[l], b_s[l])

        lambda_init = 0.8 - 0.6 * math.exp(-0.3 * l)
        lam = (jnp.exp(jnp.sum(lambda_q1[l].astype(jnp.float32) * lambda_k1[l].astype(jnp.float32)))
               - jnp.exp(jnp.sum(lambda_q2[l].astype(jnp.float32) * lambda_k2[l].astype(jnp.float32)))
               + lambda_init)
        branch_b = differential_attention(
            q.reshape(bsz, seq, ATTN_HEADS, 2, ATTN_QK_DIM),
            k.reshape(bsz, seq, ATTN_HEADS, 2, ATTN_QK_DIM),
            va.reshape(bsz, seq, ATTN_HEADS, ATTN_V_DIM),
            lam, subln_g[l], lambda_init)

        gates = jax.nn.sigmoid(gate_logits).reshape(bsz, seq, N_BRANCH, D_MODEL)
        merged = (gates[:, :, 0] * jnp.dot(branch_a, w_up_gmlp[l])
                  + gates[:, :, 1] * jnp.dot(branch_b, w_up_attn[l]))
        x = x + gt1 * jnp.dot(merged, w_o[l])

        h = rms_norm(x, norm2_g[l]) * (1.0 + sc2) + sh2
        f = jnp.dot(jax.nn.silu(jnp.dot(h, ffn_w1[l])) * jnp.dot(h, ffn_w3[l]), ffn_w2[l])
        x = x + gt2 * f
    return rms_norm(x, final_g)
```

```python
import functools
import math

import jax
import jax.numpy as jnp
from jax import lax
from jax.experimental import pallas as pl
from jax.experimental.pallas import tpu as pltpu

BF16 = jnp.bfloat16
F32 = jnp.float32

EPS = 1e-6
ROPE_THETA = 10000.0
LANES = 128
V7X_VMEM_LIMIT = 56 * 1024 * 1024


def _params(*sem):
    return pltpu.CompilerParams(dimension_semantics=sem, vmem_limit_bytes=V7X_VMEM_LIMIT)


def _sigmoid(x):
    return 1.0 / (1.0 + jnp.exp(-x))


def _adaln_kernel(c_ref, w_ref, b_ref, o_ref):
    c = c_ref[...]
    c_act = (c * _sigmoid(c)).astype(BF16)
    acc = jnp.dot(c_act, w_ref[...].astype(BF16), preferred_element_type=F32)
    o_ref[...] = acc + b_ref[...]


def _adaln(c, ada_w, ada_b, bn=1024):
    L, D, N = ada_w.shape
    B = c.shape[0]
    bn = min(bn, N)
    return pl.pallas_call(
        _adaln_kernel,
        grid=(L, N // bn),
        in_specs=[
            pl.BlockSpec((B, D), lambda l, j: (0, 0)),
            pl.BlockSpec((None, D, bn), lambda l, j: (l, 0, j)),
            pl.BlockSpec((None, 1, bn), lambda l, j: (l, 0, j)),
        ],
        out_specs=pl.BlockSpec((None, B, bn), lambda l, j: (l, 0, j)),
        out_shape=jax.ShapeDtypeStruct((L, B, N), F32),
        compiler_params=_params("parallel", "parallel"),
        name="adaln_mod",
    )(c, ada_w, ada_b.reshape(L, 1, N))


def _modulated_norm(x_ref, g_ref, sc_ref, sh_ref):
    xf = x_ref[...]
    ms = jnp.mean(xf * xf, axis=-1, keepdims=True)
    y = xf * lax.rsqrt(ms + EPS) * g_ref[...]
    return (y * (1.0 + sc_ref[0]) + sh_ref[0]).astype(BF16)


def _inproj_kernel(x_ref, g_ref, sc_ref, sh_ref, w_ref, lng_ref, lnb_ref, cos_ref, sin_ref,
                   o_ref, h_ref, *, n_heads, qk_scale):
    j = pl.program_id(1)
    bm = o_ref.shape[0]

    @pl.when(j == 0)
    def _():
        h_ref[...] = _modulated_norm(x_ref, g_ref, sc_ref, sh_ref)

    def proj():
        return jnp.dot(h_ref[...], w_ref[...], preferred_element_type=F32)

    @pl.when(j == 0)
    def _():
        o_ref[...] = jax.nn.gelu(proj()).astype(BF16)

    @pl.when(j == 1)
    def _():
        v = jax.nn.gelu(proj())
        mu = jnp.mean(v, axis=-1, keepdims=True)
        vc = v - mu
        var = jnp.mean(vc * vc, axis=-1, keepdims=True)
        o_ref[...] = (vc * lax.rsqrt(var + EPS) * lng_ref[...] + lnb_ref[...]).astype(BF16)

    def rope(scale):
        acc = proj()
        cos = cos_ref[...]
        sin = sin_ref[...]
        lane = lax.broadcasted_iota(jnp.int32, (bm, LANES), 1)
        first_half = (lane & (LANES // 4)) == 0
        for h in range(n_heads):
            xh = acc[:, h * LANES:(h + 1) * LANES]
            partner = jnp.where(first_half, pltpu.roll(xh, 3 * LANES // 4, 1),
                                pltpu.roll(xh, LANES // 4, 1))
            r = xh * cos + partner * sin
            if scale != 1.0:
                r = r * scale
            o_ref[:, h * LANES:(h + 1) * LANES] = r.astype(BF16)

    @pl.when(j == 2)
    def _():
        rope(qk_scale)

    @pl.when(j == 3)
    def _():
        rope(1.0)

    @pl.when(j == 4)
    def _():
        o_ref[...] = proj().astype(BF16)

    @pl.when(j >= 5)
    def _():
        o_ref[...] = _sigmoid(proj()).astype(BF16)


def _inproj(x2, g, sc, sh, w, lng, lnb, cos_t, sin_t, *, seq, n_heads, qk_scale, bm, bn):
    M, D = x2.shape
    N = w.shape[1]
    spb = seq // bm
    kern = functools.partial(_inproj_kernel, n_heads=n_heads, qk_scale=qk_scale)
    return pl.pallas_call(
        kern,
        grid=(M // bm, N // bn),
        in_specs=[
            pl.BlockSpec((bm, D), lambda i, j: (i, 0)),
            pl.BlockSpec((1, D), lambda i, j: (0, 0)),
            pl.BlockSpec((1, 1, D), lambda i, j: (i // spb, 0, 0)),
            pl.BlockSpec((1, 1, D), lambda i, j: (i // spb, 0, 0)),
            pl.BlockSpec((D, bn), lambda i, j: (0, j)),
            pl.BlockSpec((1, bn), lambda i, j: (0, 0)),
            pl.BlockSpec((1, bn), lambda i, j: (0, 0)),
            pl.BlockSpec((bm, LANES), lambda i, j: (i % spb, 0)),
            pl.BlockSpec((bm, LANES), lambda i, j: (i % spb, 0)),
        ],
        out_specs=pl.BlockSpec((bm, bn), lambda i, j: (i, j)),
        out_shape=jax.ShapeDtypeStruct((M, N), BF16),
        scratch_shapes=[pltpu.VMEM((bm, D), BF16)],
        compiler_params=_params("parallel", "arbitrary"),
        name="inproj",
    )(x2, g, sc, sh, w, lng, lnb, cos_t, sin_t)


def _gmlp_kernel(u_ref, v_ref, ws_ref, bs_ref, o_ref, *, chunk):
    bm, width = o_ref.shape
    for c in range(bm // chunk):
        rows = slice(c * chunk, (c + 1) * chunk)
        for g in range(width // LANES):
            cols = slice(g * LANES, (g + 1) * LANES)
            mixed = jnp.dot(ws_ref[g], v_ref[rows, cols], preferred_element_type=F32) + bs_ref[g]
            o_ref[rows, cols] = (u_ref[rows, cols].astype(F32) * mixed).astype(BF16)


def _gmlp(proj, ws, bs, *, width, chunk, bm):
    M = proj.shape[0]
    G = ws.shape[0]
    return pl.pallas_call(
        functools.partial(_gmlp_kernel, chunk=chunk),
        grid=(M // bm,),
        in_specs=[
            pl.BlockSpec((bm, width), lambda i: (i, 0)),
            pl.BlockSpec((bm, width), lambda i: (i, 1)),
            pl.BlockSpec((G, chunk, chunk), lambda i: (0, 0, 0)),
            pl.BlockSpec((G, chunk, LANES), lambda i: (0, 0, 0)),
        ],
        out_specs=pl.BlockSpec((bm, width), lambda i: (i, 0)),
        out_shape=jax.ShapeDtypeStruct((M, width), BF16),
        compiler_params=_params("parallel"),
        name="gmlp_gate",
    )(proj, proj, ws, bs)


def _attn_kernel(q_ref, k_ref, v_ref, lq1_ref, lk1_ref, lq2_ref, lk2_ref, sg_ref, o_ref,
                 *, lambda_init, bk):
    bq, vd = q_ref.shape
    dqk = vd // 2
    n_kv = k_ref.shape[0] // bk
    q = q_ref[...]
    lane = lax.broadcasted_iota(jnp.int32, (bq, vd), 1)
    zero = jnp.zeros_like(q)
    q1 = jnp.where(lane < dqk, q, zero)
    q2 = jnp.where(lane >= dqk, q, zero)
    nt = (((1,), (1,)), ((), ()))

    def update(s, m, l, acc, vb):
        m_new = jnp.maximum(m, jnp.max(s, axis=-1, keepdims=True))
        alpha = jnp.exp(m - m_new)
        p = jnp.exp(s - m_new)
        l = alpha * l + jnp.sum(p, axis=-1, keepdims=True)
        acc = alpha * acc + jnp.dot(p.astype(BF16), vb, preferred_element_type=F32)
        return m_new, l, acc

    def body(t, carry):
        m1, l1, a1, m2, l2, a2 = carry
        start = pl.multiple_of(t * bk, bk)
        kb = k_ref[pl.ds(start, bk), :]
        vb = v_ref[pl.ds(start, bk), :]
        s1 = lax.dot_general(q1, kb, nt, preferred_element_type=F32)
        m1, l1, a1 = update(s1, m1, l1, a1, vb)
        s2 = lax.dot_general(q2, kb, nt, preferred_element_type=F32)
        m2, l2, a2 = update(s2, m2, l2, a2, vb)
        return m1, l1, a1, m2, l2, a2

    m0 = jnp.full((bq, 1), -jnp.inf, F32)
    l0 = jnp.zeros((bq, 1), F32)
    a0 = jnp.zeros((bq, vd), F32)
    m1, l1, a1, m2, l2, a2 = lax.fori_loop(0, n_kv, body, (m0, l0, a0, m0, l0, a0))

    lam = (jnp.exp(jnp.sum(lq1_ref[...] * lk1_ref[...], axis=-1, keepdims=True))
           - jnp.exp(jnp.sum(lq2_ref[...] * lk2_ref[...], axis=-1, keepdims=True))
           + lambda_init)
    o = a1 / l1 - lam * (a2 / l2)
    ms = jnp.mean(o * o, axis=-1, keepdims=True)
    o = o * lax.rsqrt(ms + EPS) * sg_ref[...] * (1.0 - lambda_init)
    o_ref[...] = o.astype(BF16)


def _attention(proj, lq1, lk1, lq2, lk2, sg, *, batch, seq, n_heads, vd, q_col, k_col, v_col,
               lambda_init, bq, bk):
    M = proj.shape[0]
    nq = seq // bq
    dqk = lq1.shape[-1]
    qb, kb_, vb_ = q_col // vd, k_col // vd, v_col // vd
    small = pl.BlockSpec((1, dqk), lambda b, h, i: (0, 0))
    return pl.pallas_call(
        functools.partial(_attn_kernel, lambda_init=lambda_init, bk=bk),
        grid=(batch, n_heads, nq),
        in_specs=[
            pl.BlockSpec((bq, vd), lambda b, h, i: (b * nq + i, qb + h)),
            pl.BlockSpec((seq, vd), lambda b, h, i: (b, kb_ + h)),
            pl.BlockSpec((seq, vd), lambda b, h, i: (b, vb_ + h)),
            small, small, small, small,
            pl.BlockSpec((1, vd), lambda b, h, i: (0, 0)),
        ],
        out_specs=pl.BlockSpec((bq, vd), lambda b, h, i: (b * nq + i, h)),
        out_shape=jax.ShapeDtypeStruct((M, n_heads * vd), BF16),
        compiler_params=_params("parallel", "parallel", "parallel"),
        name="diff_attn",
    )(proj, proj, proj, lq1, lk1, lq2, lk2, sg)


def _merge_kernel(a_ref, b_ref, wa_ref, wb_ref, ga_ref, gb_ref, o_ref):
    ya = jnp.dot(a_ref[...], wa_ref[...], preferred_element_type=F32)
    yb = jnp.dot(b_ref[...], wb_ref[...], preferred_element_type=F32)
    o_ref[...] = (ga_ref[...].astype(F32) * ya + gb_ref[...].astype(F32) * yb).astype(BF16)


def _merge(a, b, wa, wb, proj, *, gate_col, bm, bn):
    M, K = a.shape
    N = wa.shape[1]
    g0 = gate_col // bn
    g1 = (gate_col + N) // bn
    return pl.pallas_call(
        _merge_kernel,
        grid=(M // bm, N // bn),
        in_specs=[
            pl.BlockSpec((bm, K), lambda i, j: (i, 0)),
            pl.BlockSpec((bm, K), lambda i, j: (i, 0)),
            pl.BlockSpec((K, bn), lambda i, j: (0, j)),
            pl.BlockSpec((K, bn), lambda i, j: (0, j)),
            pl.BlockSpec((bm, bn), lambda i, j: (i, g0 + j)),
            pl.BlockSpec((bm, bn), lambda i, j: (i, g1 + j)),
        ],
        out_specs=pl.BlockSpec((bm, bn), lambda i, j: (i, j)),
        out_shape=jax.ShapeDtypeStruct((M, N), BF16),
        compiler_params=_params("parallel", "parallel"),
        name="branch_merge",
    )(a, b, wa, wb, proj, proj)


def _resid_kernel(a_ref, w_ref, x_ref, gt_ref, o_ref):
    y = jnp.dot(a_ref[...], w_ref[...], preferred_element_type=F32)
    o_ref[...] = x_ref[...] + gt_ref[0] * y


def _proj_residual(a, w, x2, gt, *, seq, bm, bn, name):
    M, K = a.shape
    N = w.shape[1]
    spb = seq // bm
    return pl.pallas_call(
        _resid_kernel,
        grid=(M // bm, N // bn),
        in_specs=[
            pl.BlockSpec((bm, K), lambda i, j: (i, 0)),
            pl.BlockSpec((K, bn), lambda i, j: (0, j)),
            pl.BlockSpec((bm, bn), lambda i, j: (i, j)),
            pl.BlockSpec((1, 1, bn), lambda i, j: (i // spb, 0, j)),
        ],
        out_specs=pl.BlockSpec((bm, bn), lambda i, j: (i, j)),
        out_shape=jax.ShapeDtypeStruct((M, N), F32),
        input_output_aliases={2: 0},
        compiler_params=_params("parallel", "parallel"),
        name=name,
    )(a, w, x2, gt)


def _ffn_up_kernel(x_ref, g_ref, sc_ref, sh_ref, w1_ref, w3_ref, o_ref, h_ref):
    @pl.when(pl.program_id(1) == 0)
    def _():
        h_ref[...] = _modulated_norm(x_ref, g_ref, sc_ref, sh_ref)

    h = h_ref[...]
    a = jnp.dot(h, w1_ref[...], preferred_element_type=F32)
    b = jnp.dot(h, w3_ref[...], preferred_element_type=F32)
    o_ref[...] = (a * _sigmoid(a) * b).astype(BF16)


def _ffn_up(x2, g, sc, sh, w1, w3, *, seq, bm, bn):
    M, D = x2.shape
    N = w1.shape[1]
    spb = seq // bm
    return pl.pallas_call(
        _ffn_up_kernel,
        grid=(M // bm, N // bn),
        in_specs=[
            pl.BlockSpec((bm, D), lambda i, j: (i, 0)),
            pl.BlockSpec((1, D), lambda i, j: (0, 0)),
            pl.BlockSpec((1, 1, D), lambda i, j: (i // spb, 0, 0)),
            pl.BlockSpec((1, 1, D), lambda i, j: (i // spb, 0, 0)),
            pl.BlockSpec((D, bn), lambda i, j: (0, j)),
            pl.BlockSpec((D, bn), lambda i, j: (0, j)),
        ],
        out_specs=pl.BlockSpec((bm, bn), lambda i, j: (i, j)),
        out_shape=jax.ShapeDtypeStruct((M, N), BF16),
        scratch_shapes=[pltpu.VMEM((bm, D), BF16)],
        compiler_params=_params("parallel", "arbitrary"),
        name="ffn_up",
    )(x2, g, sc, sh, w1, w3)


def _final_norm_kernel(x_ref, g_ref, o_ref):
    xf = x_ref[...]
    ms = jnp.mean(xf * xf, axis=-1, keepdims=True)
    o_ref[...] = xf * lax.rsqrt(ms + EPS) * g_ref[...]


def _final_norm(x2, g, *, bm):
    M, D = x2.shape
    return pl.pallas_call(
        _final_norm_kernel,
        grid=(M // bm,),
        in_specs=[pl.BlockSpec((bm, D), lambda i: (i, 0)), pl.BlockSpec((1, D), lambda i: (0, 0))],
        out_specs=pl.BlockSpec((bm, D), lambda i: (i, 0)),
        out_shape=jax.ShapeDtypeStruct((M, D), F32),
        compiler_params=_params("parallel"),
        name="final_norm",
    )(x2, g)


def _rope_tables(seq, dim):
    pos = jnp.arange(seq, dtype=F32)
    inv_freq = ROPE_THETA ** (-jnp.arange(0, dim, 2, dtype=F32) / dim)
    ang = pos[:, None] * inv_freq[None, :]
    cos, sin = jnp.cos(ang), jnp.sin(ang)
    reps = LANES // dim
    return (jnp.tile(jnp.concatenate([cos, cos], axis=-1), (1, reps)),
            jnp.tile(jnp.concatenate([-sin, sin], axis=-1), (1, reps)))


def kernel(x, c, ada_w, ada_b, norm1_g, norm2_g, w_in, gmlp_ln_g, gmlp_ln_b, w_s, b_s, lambda_q1, lambda_k1, lambda_q2, lambda_k2, subln_g, w_up_gmlp, w_up_attn, w_o, ffn_w1, ffn_w3, ffn_w2, final_g):
    B, S, D = x.shape
    L = w_in.shape[0]
    M = B * S
    gw = gmlp_ln_g.shape[1]
    G, chunk = w_s.shape[1], w_s.shape[2]
    dqk = lambda_q1.shape[1]
    vd = subln_g.shape[1]
    aw = w_up_attn.shape[1]
    H = aw // vd
    assert gw // G == LANES and vd == LANES and 2 * dqk == vd and chunk == LANES
    q_col, k_col, v_col, gate_col = 2 * gw, 2 * gw + aw, 2 * gw + 2 * aw, 2 * gw + 3 * aw
    assert w_in.shape[2] == gate_col + 2 * D and gw == aw

    bm = min(1024, S)
    bn = 1024
    assert S % bm == 0 and gw == bn and D % bn == 0

    mod = _adaln(c, ada_w, ada_b).reshape(L, B, 6, 1, D)
    cos_t, sin_t = _rope_tables(S, dqk)
    x2 = x.reshape(M, D)

    for l in range(L):
        sh1, sc1, gt1, sh2, sc2, gt2 = [mod[l, :, t] for t in range(6)]
        lambda_init = 0.8 - 0.6 * math.exp(-0.3 * l)

        proj = _inproj(x2, norm1_g[l].reshape(1, D), sc1, sh1, w_in[l].astype(BF16),
                       gmlp_ln_g[l].reshape(1, gw), gmlp_ln_b[l].reshape(1, gw), cos_t, sin_t,
                       seq=S, n_heads=H, qk_scale=dqk ** -0.5, bm=bm, bn=bn)
        bs = jnp.broadcast_to(b_s[l][:, :, None], (G, chunk, LANES))
        branch_a = _gmlp(proj, w_s[l].astype(BF16), bs, width=gw, chunk=chunk, bm=min(512, S))
        branch_b = _attention(proj, lambda_q1[l].reshape(1, dqk), lambda_k1[l].reshape(1, dqk),
                              lambda_q2[l].reshape(1, dqk), lambda_k2[l].reshape(1, dqk),
                              subln_g[l].reshape(1, vd), batch=B, seq=S, n_heads=H, vd=vd,
                              q_col=q_col, k_col=k_col, v_col=v_col, lambda_init=lambda_init,
                              bq=min(512, S), bk=min(512, S))
        merged = _merge(branch_a, branch_b, w_up_gmlp[l].astype(BF16), w_up_attn[l].astype(BF16),
                        proj, gate_col=gate_col, bm=bm, bn=bn)
        x2 = _proj_residual(merged, w_o[l].astype(BF16), x2, gt1, seq=S, bm=bm, bn=bn,
                            name="wo_residual")

        t = _ffn_up(x2, norm2_g[l].reshape(1, D), sc2, sh2, ffn_w1[l].astype(BF16),
                    ffn_w3[l].astype(BF16), seq=S, bm=bm, bn=512)
        x2 = _proj_residual(t, ffn_w2[l].astype(BF16), x2, gt2, seq=S, bm=bm, bn=512,
                            name="ffn_down_residual")

    return _final_norm(x2, final_g.reshape(1, D), bm=min(512, S)).reshape(B, S, D)
```

```python
import functools
import math

import jax
import jax.numpy as jnp
from jax import lax
from jax.experimental import pallas as pl
from jax.experimental.pallas import tpu as pltpu

BF16 = jnp.bfloat16
F32 = jnp.float32

EPS = 1e-6
ROPE_THETA = 10000.0
LANES = 128
V7X_VMEM_LIMIT = 56 * 1024 * 1024


def _params(*sem):
    return pltpu.CompilerParams(dimension_semantics=sem, vmem_limit_bytes=V7X_VMEM_LIMIT)


def _sigmoid(x):
    return 1.0 / (1.0 + jnp.exp(-x))


def _adaln_kernel(c_ref, w_ref, b_ref, o_ref):
    c = c_ref[...]
    c_act = (c * _sigmoid(c)).astype(BF16)
    acc = jnp.dot(c_act, w_ref[...].astype(BF16), preferred_element_type=F32)
    o_ref[...] = acc + b_ref[...]


def _adaln(c, ada_w, ada_b, bn=1024):
    L, D, N = ada_w.shape
    B = c.shape[0]
    bn = min(bn, N)
    return pl.pallas_call(
        _adaln_kernel,
        grid=(L, N // bn),
        in_specs=[
            pl.BlockSpec((B, D), lambda l, j: (0, 0)),
            pl.BlockSpec((None, D, bn), lambda l, j: (l, 0, j)),
            pl.BlockSpec((None, 1, bn), lambda l, j: (l, 0, j)),
        ],
        out_specs=pl.BlockSpec((None, B, bn), lambda l, j: (l, 0, j)),
        out_shape=jax.ShapeDtypeStruct((L, B, N), F32),
        compiler_params=_params("parallel", "parallel"),
        name="adaln_mod",
    )(c, ada_w, ada_b.reshape(L, 1, N))


def _modulated_norm(x_ref, g_ref, sc_ref, sh_ref):
    xf = x_ref[...]
    ms = jnp.mean(xf * xf, axis=-1, keepdims=True)
    y = xf * lax.rsqrt(ms + EPS) * g_ref[...]
    return (y * (1.0 + sc_ref[0]) + sh_ref[0]).astype(BF16)


def _inproj_kernel(x_ref, g_ref, sc_ref, sh_ref, w_ref, lng_ref, lnb_ref, cos_ref, sin_ref,
                   o_ref, h_ref, *, n_heads, qk_scale):
    j = pl.program_id(1)
    bm = o_ref.shape[0]

    @pl.when(j == 0)
    def _():
        h_ref[...] = _modulated_norm(x_ref, g_ref, sc_ref, sh_ref)

    def proj():
        return jnp.dot(h_ref[...], w_ref[...], preferred_element_type=F32)

    @pl.when(j == 0)
    def _():
        o_ref[...] = jax.nn.gelu(proj()).astype(BF16)

    @pl.when(j == 1)
    def _():
        v = jax.nn.gelu(proj())
        mu = jnp.mean(v, axis=-1, keepdims=True)
        vc = v - mu
        var = jnp.mean(vc * vc, axis=-1, keepdims=True)
        o_ref[...] = (vc * lax.rsqrt(var + EPS) * lng_ref[...] + lnb_ref[...]).astype(BF16)

    def rope(scale):
        acc = proj()
        cos = cos_ref[...]
        sin = sin_ref[...]
        lane = lax.broadcasted_iota(jnp.int32, (bm, LANES), 1)
        first_half = (lane & (LANES // 4)) == 0
        for h in range(n_heads):
            xh = acc[:, h * LANES:(h + 1) * LANES]
            partner = jnp.where(first_half, pltpu.roll(xh, 3 * LANES // 4, 1),
                                pltpu.roll(xh, LANES // 4, 1))
            r = xh * cos + partner * sin
            if scale != 1.0:
                r = r * scale
            o_ref[:, h * LANES:(h + 1) * LANES] = r.astype(BF16)

    @pl.when(j == 2)
    def _():
        rope(qk_scale)

    @pl.when(j == 3)
    def _():
        rope(1.0)

    @pl.when(j == 4)
    def _():
        o_ref[...] = proj().astype(BF16)

    @pl.when(j >= 5)
    def _():
        o_ref[...] = _sigmoid(proj()).astype(BF16)


def _inproj(x2, g, sc, sh, w, lng, lnb, cos_t, sin_t, *, seq, n_heads, qk_scale, bm, bn):
    M, D = x2.shape
    N = w.shape[1]
    spb = seq // bm
    kern = functools.partial(_inproj_kernel, n_heads=n_heads, qk_scale=qk_scale)
    return pl.pallas_call(
        kern,
        grid=(M // bm, N // bn),
        in_specs=[
            pl.BlockSpec((bm, D), lambda i, j: (i, 0)),
            pl.BlockSpec((1, D), lambda i, j: (0, 0)),
            pl.BlockSpec((1, 1, D), lambda i, j: (i // spb, 0, 0)),
            pl.BlockSpec((1, 1, D), lambda i, j: (i // spb, 0, 0)),
            pl.BlockSpec((D, bn), lambda i, j: (0, j)),
            pl.BlockSpec((1, bn), lambda i, j: (0, 0)),
            pl.BlockSpec((1, bn), lambda i, j: (0, 0)),
            pl.BlockSpec((bm, LANES), lambda i, j: (i % spb, 0)),
            pl.BlockSpec((bm, LANES), lambda i, j: (i % spb, 0)),
        ],
        out_specs=pl.BlockSpec((bm, bn), lambda i, j: (i, j)),
        out_shape=jax.ShapeDtypeStruct((M, N), BF16),
        scratch_shapes=[pltpu.VMEM((bm, D), BF16)],
        compiler_params=_params("parallel", "arbitrary"),
        name="inproj",
    )(x2, g, sc, sh, w, lng, lnb, cos_t, sin_t)


def _gmlp_kernel(u_ref, v_ref, ws_ref, bs_ref, o_ref, *, chunk):
    bm, width = o_ref.shape
    for c in range(bm // chunk):
        rows = slice(c * chunk, (c + 1) * chunk)
        for g in range(width // LANES):
            cols = slice(g * LANES, (g + 1) * LANES)
            mixed = jnp.dot(ws_ref[g], v_ref[rows, cols], preferred_element_type=F32) + bs_ref[g]
            o_ref[rows, cols] = (u_ref[rows, cols].astype(F32) * mixed).astype(BF16)


def _gmlp(proj, ws, bs, *, width, chunk, bm):
    M = proj.shape[0]
    G = ws.shape[0]
    return pl.pallas_call(
        functools.partial(_gmlp_kernel, chunk=chunk),
        grid=(M // bm,),
        in_specs=[
            pl.BlockSpec((bm, width), lambda i: (i, 0)),
            pl.BlockSpec((bm, width), lambda i: (i, 1)),
            pl.BlockSpec((G, chunk, chunk), lambda i: (0, 0, 0)),
            pl.BlockSpec((G, chunk, LANES), lambda i: (0, 0, 0)),
        ],
        out_specs=pl.BlockSpec((bm, width), lambda i: (i, 0)),
        out_shape=jax.ShapeDtypeStruct((M, width), BF16),
        compiler_params=_params("parallel"),
        name="gmlp_gate",
    )(proj, proj, ws, bs)


def _attn_kernel(q_ref, k_ref, v_ref, lq1_ref, lk1_ref, lq2_ref, lk2_ref, sg_ref, o_ref,
                 vext_ref, s_ref, *, lambda_init, ck):
    bq, vd = q_ref.shape
    dqk = vd // 2

    @pl.when(pl.program_id(2) == 0)
    def _():
        vext_ref[:, :vd] = v_ref[...]
        vext_ref[:, vd:] = jnp.ones_like(v_ref)

    q = q_ref[...]
    lane = lax.broadcasted_iota(jnp.int32, (bq, vd), 1)
    zero = jnp.zeros_like(q)
    q_maps = (jnp.where(lane < dqk, q, zero),
              jnp.where(lane >= dqk, q, zero))
    nt = (((1,), (1,)), ((), ()))
    n_chunks = k_ref.shape[0] // ck

    part_max = [jnp.full((bq, LANES), -jnp.inf, F32) for _ in q_maps]
    for c in range(n_chunks):
        kc = k_ref[c * ck:(c + 1) * ck, :]
        for mi, qm in enumerate(q_maps):
            s = lax.dot_general(qm, kc, nt, preferred_element_type=F32)
            s_ref[mi, :, c * ck:(c + 1) * ck] = s
            for t in range(ck // LANES):
                part_max[mi] = jnp.maximum(part_max[mi], s[:, t * LANES:(t + 1) * LANES])
    row_max = [jnp.max(pm, axis=-1, keepdims=True) for pm in part_max]

    pv = [jnp.zeros((bq, 2 * vd), F32) for _ in q_maps]
    for c in range(n_chunks):
        vc = vext_ref[c * ck:(c + 1) * ck, :]
        for mi in range(len(q_maps)):
            p = jnp.exp2((s_ref[mi, :, c * ck:(c + 1) * ck] - row_max[mi]).astype(BF16))
            pv[mi] = pv[mi] + jnp.dot(p, vc, preferred_element_type=F32)
    o1 = pv[0][:, :vd] / pv[0][:, vd:]
    o2 = pv[1][:, :vd] / pv[1][:, vd:]

    lam = (jnp.exp(jnp.sum(lq1_ref[...] * lk1_ref[...], axis=-1, keepdims=True))
           - jnp.exp(jnp.sum(lq2_ref[...] * lk2_ref[...], axis=-1, keepdims=True))
           + lambda_init)
    o = o1 - lam * o2
    ms = jnp.mean(o * o, axis=-1, keepdims=True)
    o = o * lax.rsqrt(ms + EPS) * sg_ref[...] * (1.0 - lambda_init)
    o_ref[...] = o.astype(BF16)


def _attention(proj, lq1, lk1, lq2, lk2, sg, *, batch, seq, n_heads, vd, q_col, k_col, v_col,
               lambda_init, bq, ck):
    M = proj.shape[0]
    nq = seq // bq
    dqk = lq1.shape[-1]
    qb, kb_, vb_ = q_col // vd, k_col // vd, v_col // vd
    small = pl.BlockSpec((1, dqk), lambda b, h, i: (0, 0))
    return pl.pallas_call(
        functools.partial(_attn_kernel, lambda_init=lambda_init, ck=ck),
        grid=(batch, n_heads, nq),
        in_specs=[
            pl.BlockSpec((bq, vd), lambda b, h, i: (b * nq + i, qb + h)),
            pl.BlockSpec((seq, vd), lambda b, h, i: (b, kb_ + h)),
            pl.BlockSpec((seq, vd), lambda b, h, i: (b, vb_ + h)),
            small, small, small, small,
            pl.BlockSpec((1, vd), lambda b, h, i: (0, 0)),
        ],
        out_specs=pl.BlockSpec((bq, vd), lambda b, h, i: (b * nq + i, h)),
        out_shape=jax.ShapeDtypeStruct((M, n_heads * vd), BF16),
        scratch_shapes=[pltpu.VMEM((seq, 2 * vd), BF16), pltpu.VMEM((2, bq, seq), F32)],
        compiler_params=_params("parallel", "parallel", "arbitrary"),
        name="diff_attn",
    )(proj, proj, proj, lq1, lk1, lq2, lk2, sg)


def _merge_kernel(a_ref, b_ref, wa_ref, wb_ref, ga_ref, gb_ref, o_ref):
    ya = jnp.dot(a_ref[...], wa_ref[...], preferred_element_type=F32)
    yb = jnp.dot(b_ref[...], wb_ref[...], preferred_element_type=F32)
    o_ref[...] = (ga_ref[...].astype(F32) * ya + gb_ref[...].astype(F32) * yb).astype(BF16)


def _merge(a, b, wa, wb, proj, *, gate_col, bm, bn):
    M, K = a.shape
    N = wa.shape[1]
    g0 = gate_col // bn
    g1 = (gate_col + N) // bn
    return pl.pallas_call(
        _merge_kernel,
        grid=(M // bm, N // bn),
        in_specs=[
            pl.BlockSpec((bm, K), lambda i, j: (i, 0)),
            pl.BlockSpec((bm, K), lambda i, j: (i, 0)),
            pl.BlockSpec((K, bn), lambda i, j: (0, j)),
            pl.BlockSpec((K, bn), lambda i, j: (0, j)),
            pl.BlockSpec((bm, bn), lambda i, j: (i, g0 + j)),
            pl.BlockSpec((bm, bn), lambda i, j: (i, g1 + j)),
        ],
        out_specs=pl.BlockSpec((bm, bn), lambda i, j: (i, j)),
        out_shape=jax.ShapeDtypeStruct((M, N), BF16),
        compiler_params=_params("parallel", "parallel"),
        name="branch_merge",
    )(a, b, wa, wb, proj, proj)


def _resid_kernel(a_ref, w_ref, x_ref, gt_ref, o_ref):
    y = jnp.dot(a_ref[...], w_ref[...], preferred_element_type=F32)
    o_ref[...] = x_ref[...] + gt_ref[0] * y


def _proj_residual(a, w, x2, gt, *, seq, bm, bn, name):
    M, K = a.shape
    N = w.shape[1]
    spb = seq // bm
    return pl.pallas_call(
        _resid_kernel,
        grid=(M // bm, N // bn),
        in_specs=[
            pl.BlockSpec((bm, K), lambda i, j: (i, 0)),
            pl.BlockSpec((K, bn), lambda i, j: (0, j)),
            pl.BlockSpec((bm, bn), lambda i, j: (i, j)),
            pl.BlockSpec((1, 1, bn), lambda i, j: (i // spb, 0, j)),
        ],
        out_specs=pl.BlockSpec((bm, bn), lambda i, j: (i, j)),
        out_shape=jax.ShapeDtypeStruct((M, N), F32),
        input_output_aliases={2: 0},
        compiler_params=_params("parallel", "parallel"),
        name=name,
    )(a, w, x2, gt)


def _ffn_up_kernel(x_ref, g_ref, sc_ref, sh_ref, w1_ref, w3_ref, o_ref, h_ref):
    @pl.when(pl.program_id(1) == 0)
    def _():
        h_ref[...] = _modulated_norm(x_ref, g_ref, sc_ref, sh_ref)

    h = h_ref[...]
    a = jnp.dot(h, w1_ref[...], preferred_element_type=F32)
    b = jnp.dot(h, w3_ref[...], preferred_element_type=F32)
    o_ref[...] = (a * _sigmoid(a) * b).astype(BF16)


def _ffn_up(x2, g, sc, sh, w1, w3, *, seq, bm, bn):
    M, D = x2.shape
    N = w1.shape[1]
    spb = seq // bm
    return pl.pallas_call(
        _ffn_up_kernel,
        grid=(M // bm, N // bn),
        in_specs=[
            pl.BlockSpec((bm, D), lambda i, j: (i, 0)),
            pl.BlockSpec((1, D), lambda i, j: (0, 0)),
            pl.BlockSpec((1, 1, D), lambda i, j: (i // spb, 0, 0)),
            pl.BlockSpec((1, 1, D), lambda i, j: (i // spb, 0, 0)),
            pl.BlockSpec((D, bn), lambda i, j: (0, j)),
            pl.BlockSpec((D, bn), lambda i, j: (0, j)),
        ],
        out_specs=pl.BlockSpec((bm, bn), lambda i, j: (i, j)),
        out_shape=jax.ShapeDtypeStruct((M, N), BF16),
        scratch_shapes=[pltpu.VMEM((bm, D), BF16)],
        compiler_params=_params("parallel", "arbitrary"),
        name="ffn_up",
    )(x2, g, sc, sh, w1, w3)


def _final_norm_kernel(x_ref, g_ref, o_ref):
    xf = x_ref[...]
    ms = jnp.mean(xf * xf, axis=-1, keepdims=True)
    o_ref[...] = xf * lax.rsqrt(ms + EPS) * g_ref[...]


def _final_norm(x2, g, *, bm):
    M, D = x2.shape
    return pl.pallas_call(
        _final_norm_kernel,
        grid=(M // bm,),
        in_specs=[pl.BlockSpec((bm, D), lambda i: (i, 0)), pl.BlockSpec((1, D), lambda i: (0, 0))],
        out_specs=pl.BlockSpec((bm, D), lambda i: (i, 0)),
        out_shape=jax.ShapeDtypeStruct((M, D), F32),
        compiler_params=_params("parallel"),
        name="final_norm",
    )(x2, g)


def _rope_tables(seq, dim):
    pos = jnp.arange(seq, dtype=F32)
    inv_freq = ROPE_THETA ** (-jnp.arange(0, dim, 2, dtype=F32) / dim)
    ang = pos[:, None] * inv_freq[None, :]
    cos, sin = jnp.cos(ang), jnp.sin(ang)
    reps = LANES // dim
    return (jnp.tile(jnp.concatenate([cos, cos], axis=-1), (1, reps)),
            jnp.tile(jnp.concatenate([-sin, sin], axis=-1), (1, reps)))


def kernel(x, c, ada_w, ada_b, norm1_g, norm2_g, w_in, gmlp_ln_g, gmlp_ln_b, w_s, b_s, lambda_q1, lambda_k1, lambda_q2, lambda_k2, subln_g, w_up_gmlp, w_up_attn, w_o, ffn_w1, ffn_w3, ffn_w2, final_g):
    B, S, D = x.shape
    L = w_in.shape[0]
    M = B * S
    gw = gmlp_ln_g.shape[1]
    G, chunk = w_s.shape[1], w_s.shape[2]
    dqk = lambda_q1.shape[1]
    vd = subln_g.shape[1]
    aw = w_up_attn.shape[1]
    H = aw // vd
    assert gw // G == LANES and vd == LANES and 2 * dqk == vd and chunk == LANES
    q_col, k_col, v_col, gate_col = 2 * gw, 2 * gw + aw, 2 * gw + 2 * aw, 2 * gw + 3 * aw
    assert w_in.shape[2] == gate_col + 2 * D and gw == aw

    bm = min(1024, S)
    bn = 1024
    assert S % bm == 0 and gw == bn and D % bn == 0

    mod = _adaln(c, ada_w, ada_b).reshape(L, B, 6, 1, D)
    cos_t, sin_t = _rope_tables(S, dqk)
    x2 = x.reshape(M, D)

    for l in range(L):
        sh1, sc1, gt1, sh2, sc2, gt2 = [mod[l, :, t] for t in range(6)]
        lambda_init = 0.8 - 0.6 * math.exp(-0.3 * l)

        proj = _inproj(x2, norm1_g[l].reshape(1, D), sc1, sh1, w_in[l].astype(BF16),
                       gmlp_ln_g[l].reshape(1, gw), gmlp_ln_b[l].reshape(1, gw), cos_t, sin_t,
                       seq=S, n_heads=H, qk_scale=dqk ** -0.5 * math.log2(math.e), bm=bm, bn=bn)
        bs = jnp.broadcast_to(b_s[l][:, :, None], (G, chunk, LANES))
        branch_a = _gmlp(proj, w_s[l].astype(BF16), bs, width=gw, chunk=chunk, bm=min(512, S))
        branch_b = _attention(proj, lambda_q1[l].reshape(1, dqk), lambda_k1[l].reshape(1, dqk),
                              lambda_q2[l].reshape(1, dqk), lambda_k2[l].reshape(1, dqk),
                              subln_g[l].reshape(1, vd), batch=B, seq=S, n_heads=H, vd=vd,
                              q_col=q_col, k_col=k_col, v_col=v_col, lambda_init=lambda_init,
                              bq=min(512, S), ck=min(512, S))
        merged = _merge(branch_a, branch_b, w_up_gmlp[l].astype(BF16), w_up_attn[l].astype(BF16),
                        proj, gate_col=gate_col, bm=bm, bn=bn)
        x2 = _proj_residual(merged, w_o[l].astype(BF16), x2, gt1, seq=S, bm=bm, bn=bn,
                            name="wo_residual")

        t = _ffn_up(x2, norm2_g[l].reshape(1, D), sc2, sh2, ffn_w1[l].astype(BF16),
                    ffn_w3[l].astype(BF16), seq=S, bm=bm, bn=512)
        x2 = _proj_residual(t, ffn_w2[l].astype(BF16), x2, gt2, seq=S, bm=bm, bn=512,
                            name="ffn_down_residual")

    return _final_norm(x2, final_g.reshape(1, D), bm=min(512, S)).reshape(B, S, D)
```

```python
import functools
import math

import jax
import jax.numpy as jnp
from jax import lax
from jax.experimental import pallas as pl
from jax.experimental.pallas import tpu as pltpu

BF16 = jnp.bfloat16
F32 = jnp.float32

EPS = 1e-6
ROPE_THETA = 10000.0
LANES = 128
V7X_VMEM_LIMIT = 56 * 1024 * 1024


def _params(*sem):
    return pltpu.CompilerParams(dimension_semantics=sem, vmem_limit_bytes=V7X_VMEM_LIMIT)


def _sigmoid(x):
    return 1.0 / (1.0 + jnp.exp(-x))


def _adaln_kernel(c_ref, w_ref, b_ref, o_ref):
    c = c_ref[...]
    c_act = (c * _sigmoid(c)).astype(BF16)
    acc = jnp.dot(c_act, w_ref[...].astype(BF16), preferred_element_type=F32)
    o_ref[...] = acc + b_ref[...]


def _adaln(c, ada_w, ada_b, bn=1024):
    L, D, N = ada_w.shape
    B = c.shape[0]
    bn = min(bn, N)
    return pl.pallas_call(
        _adaln_kernel,
        grid=(L, N // bn),
        in_specs=[
            pl.BlockSpec((B, D), lambda l, j: (0, 0)),
            pl.BlockSpec((None, D, bn), lambda l, j: (l, 0, j)),
            pl.BlockSpec((None, 1, bn), lambda l, j: (l, 0, j)),
        ],
        out_specs=pl.BlockSpec((None, B, bn), lambda l, j: (l, 0, j)),
        out_shape=jax.ShapeDtypeStruct((L, B, N), F32),
        compiler_params=_params("parallel", "parallel"),
        name="adaln_mod",
    )(c, ada_w, ada_b.reshape(L, 1, N))


def _modulated_norm(x_ref, g_ref, sc_ref, sh_ref):
    xf = x_ref[...]
    ms = jnp.mean(xf * xf, axis=-1, keepdims=True)
    y = xf * lax.rsqrt(ms + EPS) * g_ref[...]
    return (y * (1.0 + sc_ref[0]) + sh_ref[0]).astype(BF16)


def _inproj_kernel(x_ref, g_ref, sc_ref, sh_ref, w_ref, lng_ref, lnb_ref, cos_ref, sin_ref,
                   o_ref, h_ref, *, n_heads, qk_scale):
    j = pl.program_id(1)
    bm = o_ref.shape[0]

    @pl.when(j == 0)
    def _():
        h_ref[...] = _modulated_norm(x_ref, g_ref, sc_ref, sh_ref)

    def proj():
        return jnp.dot(h_ref[...], w_ref[...], preferred_element_type=F32)

    @pl.when(j == 0)
    def _():
        o_ref[...] = jax.nn.gelu(proj()).astype(BF16)

    @pl.when(j == 1)
    def _():
        v = jax.nn.gelu(proj())
        mu = jnp.mean(v, axis=-1, keepdims=True)
        vc = v - mu
        var = jnp.mean(vc * vc, axis=-1, keepdims=True)
        o_ref[...] = (vc * lax.rsqrt(var + EPS) * lng_ref[...] + lnb_ref[...]).astype(BF16)

    def rope(scale):
        acc = proj()
        cos = cos_ref[...]
        sin = sin_ref[...]
        lane = lax.broadcasted_iota(jnp.int32, (bm, LANES), 1)
        first_half = (lane & (LANES // 4)) == 0
        for h in range(n_heads):
            xh = acc[:, h * LANES:(h + 1) * LANES]
            partner = jnp.where(first_half, pltpu.roll(xh, 3 * LANES // 4, 1),
                                pltpu.roll(xh, LANES // 4, 1))
            r = xh * cos + partner * sin
            if scale != 1.0:
                r = r * scale
            o_ref[:, h * LANES:(h + 1) * LANES] = r.astype(BF16)

    @pl.when(j == 2)
    def _():
        rope(qk_scale)

    @pl.when(j == 3)
    def _():
        rope(1.0)

    @pl.when(j == 4)
    def _():
        o_ref[...] = proj().astype(BF16)

    @pl.when(j >= 5)
    def _():
        o_ref[...] = _sigmoid(proj()).astype(BF16)


def _inproj(x2, g, sc, sh, w, lng, lnb, cos_t, sin_t, *, seq, n_heads, qk_scale, bm, bn):
    M, D = x2.shape
    N = w.shape[1]
    spb = seq // bm
    kern = functools.partial(_inproj_kernel, n_heads=n_heads, qk_scale=qk_scale)
    return pl.pallas_call(
        kern,
        grid=(M // bm, N // bn),
        in_specs=[
            pl.BlockSpec((bm, D), lambda i, j: (i, 0)),
            pl.BlockSpec((1, D), lambda i, j: (0, 0)),
            pl.BlockSpec((1, 1, D), lambda i, j: (i // spb, 0, 0)),
            pl.BlockSpec((1, 1, D), lambda i, j: (i // spb, 0, 0)),
            pl.BlockSpec((D, bn), lambda i, j: (0, j)),
            pl.BlockSpec((1, bn), lambda i, j: (0, 0)),
            pl.BlockSpec((1, bn), lambda i, j: (0, 0)),
            pl.BlockSpec((bm, LANES), lambda i, j: (i % spb, 0)),
            pl.BlockSpec((bm, LANES), lambda i, j: (i % spb, 0)),
        ],
        out_specs=pl.BlockSpec((bm, bn), lambda i, j: (i, j)),
        out_shape=jax.ShapeDtypeStruct((M, N), BF16),
        scratch_shapes=[pltpu.VMEM((bm, D), BF16)],
        compiler_params=_params("parallel", "arbitrary"),
        name="inproj",
    )(x2, g, sc, sh, w, lng, lnb, cos_t, sin_t)


def _gmlp_kernel(u_ref, v_ref, ws_ref, bs_ref, o_ref, *, chunk):
    bm, width = o_ref.shape
    for c in range(bm // chunk):
        rows = slice(c * chunk, (c + 1) * chunk)
        for g in range(width // LANES):
            cols = slice(g * LANES, (g + 1) * LANES)
            mixed = jnp.dot(ws_ref[g], v_ref[rows, cols], preferred_element_type=F32) + bs_ref[g]
            o_ref[rows, cols] = (u_ref[rows, cols].astype(F32) * mixed).astype(BF16)


def _gmlp(proj, ws, bs, *, width, chunk, bm):
    M = proj.shape[0]
    G = ws.shape[0]
    return pl.pallas_call(
        functools.partial(_gmlp_kernel, chunk=chunk),
        grid=(M // bm,),
        in_specs=[
            pl.BlockSpec((bm, width), lambda i: (i, 0)),
            pl.BlockSpec((bm, width), lambda i: (i, 1)),
            pl.BlockSpec((G, chunk, chunk), lambda i: (0, 0, 0)),
            pl.BlockSpec((G, chunk, LANES), lambda i: (0, 0, 0)),
        ],
        out_specs=pl.BlockSpec((bm, width), lambda i: (i, 0)),
        out_shape=jax.ShapeDtypeStruct((M, width), BF16),
        compiler_params=_params("parallel"),
        name="gmlp_gate",
    )(proj, proj, ws, bs)


def _attn_kernel(q_ref, k_ref, v_ref, lq1_ref, lk1_ref, lq2_ref, lk2_ref, sg_ref, o_ref,
                 vext_ref, s_ref, *, lambda_init, ck):
    bq, vd = q_ref.shape
    dqk = vd // 2

    @pl.when(pl.program_id(2) == 0)
    def _():
        vext_ref[:, :vd] = v_ref[...]
        vext_ref[:, vd:] = jnp.ones_like(v_ref)

    q = q_ref[...]
    lane = lax.broadcasted_iota(jnp.int32, (bq, vd), 1)
    zero = jnp.zeros_like(q)
    q_maps = (jnp.where(lane < dqk, q, zero),
              jnp.where(lane >= dqk, q, zero))
    nt = (((1,), (1,)), ((), ()))
    n_chunks = k_ref.shape[0] // ck

    part_max = [jnp.full((bq, LANES), -jnp.inf, F32) for _ in q_maps]
    for c in range(n_chunks):
        kc = k_ref[c * ck:(c + 1) * ck, :]
        for mi, qm in enumerate(q_maps):
            s = lax.dot_general(qm, kc, nt, preferred_element_type=F32)
            s_ref[mi, :, c * ck:(c + 1) * ck] = s
            for t in range(ck // LANES):
                part_max[mi] = jnp.maximum(part_max[mi], s[:, t * LANES:(t + 1) * LANES])
    row_max = [jnp.max(pm, axis=-1, keepdims=True) for pm in part_max]

    pv = [jnp.zeros((bq, 2 * vd), F32) for _ in q_maps]
    for c in range(n_chunks):
        vc = vext_ref[c * ck:(c + 1) * ck, :]
        for mi in range(len(q_maps)):
            p = jnp.exp2((s_ref[mi, :, c * ck:(c + 1) * ck] - row_max[mi]).astype(BF16))
            pv[mi] = pv[mi] + jnp.dot(p, vc, preferred_element_type=F32)
    o1 = pv[0][:, :vd] / pv[0][:, vd:]
    o2 = pv[1][:, :vd] / pv[1][:, vd:]

    lam = (jnp.exp(jnp.sum(lq1_ref[...] * lk1_ref[...], axis=-1, keepdims=True))
           - jnp.exp(jnp.sum(lq2_ref[...] * lk2_ref[...], axis=-1, keepdims=True))
           + lambda_init)
    o = o1 - lam * o2
    ms = jnp.mean(o * o, axis=-1, keepdims=True)
    o = o * lax.rsqrt(ms + EPS) * sg_ref[...] * (1.0 - lambda_init)
    o_ref[...] = o.astype(BF16)


def _attention(proj, lq1, lk1, lq2, lk2, sg, *, batch, seq, n_heads, vd, q_col, k_col, v_col,
               lambda_init, bq, ck):
    M = proj.shape[0]
    nq = seq // bq
    dqk = lq1.shape[-1]
    qb, kb_, vb_ = q_col // vd, k_col // vd, v_col // vd
    small = pl.BlockSpec((1, dqk), lambda b, h, i: (0, 0))
    return pl.pallas_call(
        functools.partial(_attn_kernel, lambda_init=lambda_init, ck=ck),
        grid=(batch, n_heads, nq),
        in_specs=[
            pl.BlockSpec((bq, vd), lambda b, h, i: (b * nq + i, qb + h)),
            pl.BlockSpec((seq, vd), lambda b, h, i: (b, kb_ + h)),
            pl.BlockSpec((seq, vd), lambda b, h, i: (b, vb_ + h)),
            small, small, small, small,
            pl.BlockSpec((1, vd), lambda b, h, i: (0, 0)),
        ],
        out_specs=pl.BlockSpec((bq, vd), lambda b, h, i: (b * nq + i, h)),
        out_shape=jax.ShapeDtypeStruct((M, n_heads * vd), BF16),
        scratch_shapes=[pltpu.VMEM((seq, 2 * vd), BF16), pltpu.VMEM((2, bq, seq), F32)],
        compiler_params=_params("parallel", "parallel", "arbitrary"),
        name="diff_attn",
    )(proj, proj, proj, lq1, lk1, lq2, lk2, sg)


def _merge_kernel(a_ref, b_ref, wa_ref, wb_ref, *refs, n_gate_blocks):
    ga_refs, gb_refs, o_ref = refs[:n_gate_blocks], refs[n_gate_blocks:-1], refs[-1]
    ya = jnp.dot(a_ref[...], wa_ref[...], preferred_element_type=F32)
    yb = jnp.dot(b_ref[...], wb_ref[...], preferred_element_type=F32)
    gbw = ga_refs[0].shape[1]
    for t in range(n_gate_blocks):
        cols = slice(t * gbw, (t + 1) * gbw)
        o_ref[:, cols] = (ga_refs[t][...].astype(F32) * ya[:, cols]
                          + gb_refs[t][...].astype(F32) * yb[:, cols]).astype(BF16)


def _merge(a, b, wa, wb, proj, *, gate_col, gbw, bm):
    M, K = a.shape
    N = wa.shape[1]
    assert gate_col % gbw == 0 and N % gbw == 0
    n_gb = N // gbw
    ga0, gb0 = gate_col // gbw, (gate_col + N) // gbw
    gate_specs = [pl.BlockSpec((bm, gbw), functools.partial(lambda i, blk: (i, blk), blk=g0 + t))
                  for g0 in (ga0, gb0) for t in range(n_gb)]
    return pl.pallas_call(
        functools.partial(_merge_kernel, n_gate_blocks=n_gb),
        grid=(M // bm,),
        in_specs=[
            pl.BlockSpec((bm, K), lambda i: (i, 0)),
            pl.BlockSpec((bm, K), lambda i: (i, 0)),
            pl.BlockSpec((K, N), lambda i: (0, 0)),
            pl.BlockSpec((K, N), lambda i: (0, 0)),
        ] + gate_specs,
        out_specs=pl.BlockSpec((bm, N), lambda i: (i, 0)),
        out_shape=jax.ShapeDtypeStruct((M, N), BF16),
        compiler_params=_params("parallel"),
        name="branch_merge",
    )(a, b, wa, wb, *([proj] * (2 * n_gb)))


def _resid_kernel(a_ref, w_ref, x_ref, gt_ref, o_ref):
    y = jnp.dot(a_ref[...], w_ref[...], preferred_element_type=F32)
    o_ref[...] = x_ref[...] + gt_ref[0] * y


def _proj_residual(a, w, x2, gt, *, seq, bm, bn, name):
    M, K = a.shape
    N = w.shape[1]
    spb = seq // bm
    return pl.pallas_call(
        _resid_kernel,
        grid=(M // bm, N // bn),
        in_specs=[
            pl.BlockSpec((bm, K), lambda i, j: (i, 0)),
            pl.BlockSpec((K, bn), lambda i, j: (0, j)),
            pl.BlockSpec((bm, bn), lambda i, j: (i, j)),
            pl.BlockSpec((1, 1, bn), lambda i, j: (i // spb, 0, j)),
        ],
        out_specs=pl.BlockSpec((bm, bn), lambda i, j: (i, j)),
        out_shape=jax.ShapeDtypeStruct((M, N), F32),
        input_output_aliases={2: 0},
        compiler_params=_params("parallel", "parallel"),
        name=name,
    )(a, w, x2, gt)


def _ffn_up_kernel(x_ref, g_ref, sc_ref, sh_ref, w1_ref, w3_ref, o_ref, h_ref):
    @pl.when(pl.program_id(1) == 0)
    def _():
        h_ref[...] = _modulated_norm(x_ref, g_ref, sc_ref, sh_ref)

    h = h_ref[...]
    a = jnp.dot(h, w1_ref[...], preferred_element_type=F32)
    b = jnp.dot(h, w3_ref[...], preferred_element_type=F32)
    o_ref[...] = (a * _sigmoid(a) * b).astype(BF16)


def _ffn_up(x2, g, sc, sh, w1, w3, *, seq, bm, bn):
    M, D = x2.shape
    N = w1.shape[1]
    spb = seq // bm
    return pl.pallas_call(
        _ffn_up_kernel,
        grid=(M // bm, N // bn),
        in_specs=[
            pl.BlockSpec((bm, D), lambda i, j: (i, 0)),
            pl.BlockSpec((1, D), lambda i, j: (0, 0)),
            pl.BlockSpec((1, 1, D), lambda i, j: (i // spb, 0, 0)),
            pl.BlockSpec((1, 1, D), lambda i, j: (i // spb, 0, 0)),
            pl.BlockSpec((D, bn), lambda i, j: (0, j)),
            pl.BlockSpec((D, bn), lambda i, j: (0, j)),
        ],
        out_specs=pl.BlockSpec((bm, bn), lambda i, j: (i, j)),
        out_shape=jax.ShapeDtypeStruct((M, N), BF16),
        scratch_shapes=[pltpu.VMEM((bm, D), BF16)],
        compiler_params=_params("parallel", "arbitrary"),
        name="ffn_up",
    )(x2, g, sc, sh, w1, w3)


def _final_norm_kernel(x_ref, g_ref, o_ref):
    xf = x_ref[...]
    ms = jnp.mean(xf * xf, axis=-1, keepdims=True)
    o_ref[...] = xf * lax.rsqrt(ms + EPS) * g_ref[...]


def _final_norm(x2, g, *, bm):
    M, D = x2.shape
    return pl.pallas_call(
        _final_norm_kernel,
        grid=(M // bm,),
        in_specs=[pl.BlockSpec((bm, D), lambda i: (i, 0)), pl.BlockSpec((1, D), lambda i: (0, 0))],
        out_specs=pl.BlockSpec((bm, D), lambda i: (i, 0)),
        out_shape=jax.ShapeDtypeStruct((M, D), F32),
        compiler_params=_params("parallel"),
        name="final_norm",
    )(x2, g)


def _rope_tables(seq, dim):
    pos = jnp.arange(seq, dtype=F32)
    inv_freq = ROPE_THETA ** (-jnp.arange(0, dim, 2, dtype=F32) / dim)
    ang = pos[:, None] * inv_freq[None, :]
    cos, sin = jnp.cos(ang), jnp.sin(ang)
    reps = LANES // dim
    return (jnp.tile(jnp.concatenate([cos, cos], axis=-1), (1, reps)),
            jnp.tile(jnp.concatenate([-sin, sin], axis=-1), (1, reps)))


def kernel(x, c, ada_w, ada_b, norm1_g, norm2_g, w_in, gmlp_ln_g, gmlp_ln_b, w_s, b_s, lambda_q1, lambda_k1, lambda_q2, lambda_k2, subln_g, w_up_gmlp, w_up_attn, w_o, ffn_w1, ffn_w3, ffn_w2, final_g):
    B, S, D = x.shape
    L = w_in.shape[0]
    M = B * S
    gw = gmlp_ln_g.shape[1]
    G, chunk = w_s.shape[1], w_s.shape[2]
    dqk = lambda_q1.shape[1]
    vd = subln_g.shape[1]
    aw = w_up_attn.shape[1]
    H = aw // vd
    assert gw // G == LANES and vd == LANES and 2 * dqk == vd and chunk == LANES
    q_col, k_col, v_col, gate_col = 2 * gw, 2 * gw + aw, 2 * gw + 2 * aw, 2 * gw + 3 * aw
    assert w_in.shape[2] == gate_col + 2 * D and gw == aw

    bm = min(1024, S)
    bn = 1024
    bm_res = min(512, S)
    assert S % bm == 0 and gw == bn and D % bn == 0

    mod = _adaln(c, ada_w, ada_b).reshape(L, B, 6, 1, D)
    cos_t, sin_t = _rope_tables(S, dqk)
    x2 = x.reshape(M, D)

    for l in range(L):
        sh1, sc1, gt1, sh2, sc2, gt2 = [mod[l, :, t] for t in range(6)]
        lambda_init = 0.8 - 0.6 * math.exp(-0.3 * l)

        proj = _inproj(x2, norm1_g[l].reshape(1, D), sc1, sh1, w_in[l].astype(BF16),
                       gmlp_ln_g[l].reshape(1, gw), gmlp_ln_b[l].reshape(1, gw), cos_t, sin_t,
                       seq=S, n_heads=H, qk_scale=dqk ** -0.5 * math.log2(math.e), bm=bm, bn=bn)
        bs = jnp.broadcast_to(b_s[l][:, :, None], (G, chunk, LANES))
        branch_a = _gmlp(proj, w_s[l].astype(BF16), bs, width=gw, chunk=chunk, bm=min(512, S))
        branch_b = _attention(proj, lambda_q1[l].reshape(1, dqk), lambda_k1[l].reshape(1, dqk),
                              lambda_q2[l].reshape(1, dqk), lambda_k2[l].reshape(1, dqk),
                              subln_g[l].reshape(1, vd), batch=B, seq=S, n_heads=H, vd=vd,
                              q_col=q_col, k_col=k_col, v_col=v_col, lambda_init=lambda_init,
                              bq=min(1024, S), ck=min(512, S))
        merged = _merge(branch_a, branch_b, w_up_gmlp[l].astype(BF16), w_up_attn[l].astype(BF16),
                        proj, gate_col=gate_col, gbw=bn, bm=bm_res)
        x2 = _proj_residual(merged, w_o[l].astype(BF16), x2, gt1, seq=S, bm=bm_res, bn=D,
                            name="wo_residual")

        t = _ffn_up(x2, norm2_g[l].reshape(1, D), sc2, sh2, ffn_w1[l].astype(BF16),
                    ffn_w3[l].astype(BF16), seq=S, bm=bm, bn=512)
        x2 = _proj_residual(t, ffn_w2[l].astype(BF16), x2, gt2, seq=S, bm=bm, bn=512,
                            name="ffn_down_residual")

    return _final_norm(x2, final_g.reshape(1, D), bm=min(512, S)).reshape(B, S, D)
```

```python
import functools
import math

import jax
import jax.numpy as jnp
from jax import lax
from jax.experimental import pallas as pl
from jax.experimental.pallas import tpu as pltpu

BF16 = jnp.bfloat16
F32 = jnp.float32

EPS = 1e-6
ROPE_THETA = 10000.0
LANES = 128
V7X_VMEM_LIMIT = 56 * 1024 * 1024


def _params(*sem):
    return pltpu.CompilerParams(dimension_semantics=sem, vmem_limit_bytes=V7X_VMEM_LIMIT)


def _sigmoid(x):
    return 1.0 / (1.0 + jnp.exp(-x))


def _adaln_kernel(c_ref, w_ref, b_ref, o_ref):
    c = c_ref[...]
    c_act = (c * _sigmoid(c)).astype(BF16)
    acc = jnp.dot(c_act, w_ref[...].astype(BF16), preferred_element_type=F32)
    o_ref[...] = acc + b_ref[...]


def _adaln(c, ada_w, ada_b, bn=1024):
    L, D, N = ada_w.shape
    B = c.shape[0]
    bn = min(bn, N)
    return pl.pallas_call(
        _adaln_kernel,
        grid=(L, N // bn),
        in_specs=[
            pl.BlockSpec((B, D), lambda l, j: (0, 0)),
            pl.BlockSpec((None, D, bn), lambda l, j: (l, 0, j)),
            pl.BlockSpec((None, 1, bn), lambda l, j: (l, 0, j)),
        ],
        out_specs=pl.BlockSpec((None, B, bn), lambda l, j: (l, 0, j)),
        out_shape=jax.ShapeDtypeStruct((L, B, N), F32),
        compiler_params=_params("parallel", "parallel"),
        name="adaln_mod",
    )(c, ada_w, ada_b.reshape(L, 1, N))


def _modulated_norm(x_ref, g_ref, sc_ref, sh_ref):
    xf = x_ref[...]
    ms = jnp.mean(xf * xf, axis=-1, keepdims=True)
    y = xf * lax.rsqrt(ms + EPS) * g_ref[...]
    return (y * (1.0 + sc_ref[0]) + sh_ref[0]).astype(BF16)


def _inproj_kernel(x_ref, g_ref, sc_ref, sh_ref, w_ref, lng_ref, lnb_ref, cos_ref, sin_ref,
                   o_ref, h_ref, *, n_heads, qk_scale):
    j = pl.program_id(1)
    bm = o_ref.shape[0]

    @pl.when(j == 0)
    def _():
        h_ref[...] = _modulated_norm(x_ref, g_ref, sc_ref, sh_ref)

    def proj():
        return jnp.dot(h_ref[...], w_ref[...], preferred_element_type=F32)

    @pl.when(j == 0)
    def _():
        o_ref[...] = jax.nn.gelu(proj()).astype(BF16)

    @pl.when(j == 1)
    def _():
        v = jax.nn.gelu(proj())
        mu = jnp.mean(v, axis=-1, keepdims=True)
        vc = v - mu
        var = jnp.mean(vc * vc, axis=-1, keepdims=True)
        o_ref[...] = (vc * lax.rsqrt(var + EPS) * lng_ref[...] + lnb_ref[...]).astype(BF16)

    def rope(scale):
        acc = proj()
        cos = cos_ref[...]
        sin = sin_ref[...]
        lane = lax.broadcasted_iota(jnp.int32, (bm, LANES), 1)
        first_half = (lane & (LANES // 4)) == 0
        for h in range(n_heads):
            xh = acc[:, h * LANES:(h + 1) * LANES]
            partner = jnp.where(first_half, pltpu.roll(xh, 3 * LANES // 4, 1),
                                pltpu.roll(xh, LANES // 4, 1))
            r = xh * cos + partner * sin
            if scale != 1.0:
                r = r * scale
            o_ref[:, h * LANES:(h + 1) * LANES] = r.astype(BF16)

    @pl.when(j == 2)
    def _():
        rope(qk_scale)

    @pl.when(j == 3)
    def _():
        rope(1.0)

    @pl.when(j == 4)
    def _():
        o_ref[...] = proj().astype(BF16)

    @pl.when(j >= 5)
    def _():
        o_ref[...] = _sigmoid(proj()).astype(BF16)


def _inproj(x2, g, sc, sh, w, lng, lnb, cos_t, sin_t, *, layer, seq, n_heads, qk_scale, bm, bn):
    M, D = x2.shape
    N = w.shape[2]
    spb = seq // bm
    kern = functools.partial(_inproj_kernel, n_heads=n_heads, qk_scale=qk_scale)
    return pl.pallas_call(
        kern,
        grid=(M // bm, N // bn),
        in_specs=[
            pl.BlockSpec((bm, D), lambda i, j: (i, 0)),
            pl.BlockSpec((1, D), lambda i, j: (0, 0)),
            pl.BlockSpec((1, 1, D), lambda i, j: (i // spb, 0, 0)),
            pl.BlockSpec((1, 1, D), lambda i, j: (i // spb, 0, 0)),
            pl.BlockSpec((None, D, bn), lambda i, j: (layer, 0, j)),
            pl.BlockSpec((1, bn), lambda i, j: (0, 0)),
            pl.BlockSpec((1, bn), lambda i, j: (0, 0)),
            pl.BlockSpec((bm, LANES), lambda i, j: (i % spb, 0)),
            pl.BlockSpec((bm, LANES), lambda i, j: (i % spb, 0)),
        ],
        out_specs=pl.BlockSpec((bm, bn), lambda i, j: (i, j)),
        out_shape=jax.ShapeDtypeStruct((M, N), BF16),
        scratch_shapes=[pltpu.VMEM((bm, D), BF16)],
        compiler_params=_params("parallel", "arbitrary"),
        name="inproj",
    )(x2, g, sc, sh, w, lng, lnb, cos_t, sin_t)


def _gmlp_kernel(u_ref, v_ref, ws_ref, bs_ref, o_ref, *, chunk):
    bm, width = o_ref.shape
    for c in range(bm // chunk):
        rows = slice(c * chunk, (c + 1) * chunk)
        for g in range(width // LANES):
            cols = slice(g * LANES, (g + 1) * LANES)
            mixed = jnp.dot(ws_ref[g], v_ref[rows, cols], preferred_element_type=F32) + bs_ref[g]
            o_ref[rows, cols] = (u_ref[rows, cols].astype(F32) * mixed).astype(BF16)


def _gmlp(proj, ws, bs, *, layer, width, chunk, bm):
    M = proj.shape[0]
    G = ws.shape[1]
    return pl.pallas_call(
        functools.partial(_gmlp_kernel, chunk=chunk),
        grid=(M // bm,),
        in_specs=[
            pl.BlockSpec((bm, width), lambda i: (i, 0)),
            pl.BlockSpec((bm, width), lambda i: (i, 1)),
            pl.BlockSpec((None, G, chunk, chunk), lambda i: (layer, 0, 0, 0)),
            pl.BlockSpec((G, chunk, LANES), lambda i: (0, 0, 0)),
        ],
        out_specs=pl.BlockSpec((bm, width), lambda i: (i, 0)),
        out_shape=jax.ShapeDtypeStruct((M, width), BF16),
        compiler_params=_params("parallel"),
        name="gmlp_gate",
    )(proj, proj, ws, bs)


def _attn_kernel(q_ref, k_ref, v_ref, lq1_ref, lk1_ref, lq2_ref, lk2_ref, sg_ref, o_ref,
                 vext_ref, s_ref, *, lambda_init, ck):
    bq, vd = q_ref.shape
    dqk = vd // 2

    @pl.when(pl.program_id(2) == 0)
    def _():
        vext_ref[:, :vd] = v_ref[...]
        vext_ref[:, vd:] = jnp.ones_like(v_ref)

    q = q_ref[...]
    lane = lax.broadcasted_iota(jnp.int32, (bq, vd), 1)
    zero = jnp.zeros_like(q)
    q_maps = (jnp.where(lane < dqk, q, zero),
              jnp.where(lane >= dqk, q, zero))
    nt = (((1,), (1,)), ((), ()))
    n_chunks = k_ref.shape[0] // ck

    part_max = [jnp.full((bq, LANES), -jnp.inf, F32) for _ in q_maps]
    for c in range(n_chunks):
        kc = k_ref[c * ck:(c + 1) * ck, :]
        for mi, qm in enumerate(q_maps):
            s = lax.dot_general(qm, kc, nt, preferred_element_type=F32)
            s_ref[mi, :, c * ck:(c + 1) * ck] = s
            for t in range(ck // LANES):
                part_max[mi] = jnp.maximum(part_max[mi], s[:, t * LANES:(t + 1) * LANES])
    row_max = [jnp.max(pm, axis=-1, keepdims=True) for pm in part_max]

    pv = [jnp.zeros((bq, 2 * vd), F32) for _ in q_maps]
    for c in range(n_chunks):
        vc = vext_ref[c * ck:(c + 1) * ck, :]
        for mi in range(len(q_maps)):
            p = jnp.exp2((s_ref[mi, :, c * ck:(c + 1) * ck] - row_max[mi]).astype(BF16))
            pv[mi] = pv[mi] + jnp.dot(p, vc, preferred_element_type=F32)
    o1 = pv[0][:, :vd] / pv[0][:, vd:]
    o2 = pv[1][:, :vd] / pv[1][:, vd:]

    lam = (jnp.exp(jnp.sum(lq1_ref[...] * lk1_ref[...], axis=-1, keepdims=True))
           - jnp.exp(jnp.sum(lq2_ref[...] * lk2_ref[...], axis=-1, keepdims=True))
           + lambda_init)
    o = o1 - lam * o2
    ms = jnp.mean(o * o, axis=-1, keepdims=True)
    o = o * lax.rsqrt(ms + EPS) * sg_ref[...] * (1.0 - lambda_init)
    o_ref[...] = o.astype(BF16)


def _attention(proj, lq1, lk1, lq2, lk2, sg, *, batch, seq, n_heads, vd, q_col, k_col, v_col,
               lambda_init, bq, ck):
    M = proj.shape[0]
    nq = seq // bq
    dqk = lq1.shape[-1]
    qb, kb_, vb_ = q_col // vd, k_col // vd, v_col // vd
    small = pl.BlockSpec((1, dqk), lambda b, h, i: (0, 0))
    return pl.pallas_call(
        functools.partial(_attn_kernel, lambda_init=lambda_init, ck=ck),
        grid=(batch, n_heads, nq),
        in_specs=[
            pl.BlockSpec((bq, vd), lambda b, h, i: (b * nq + i, qb + h)),
            pl.BlockSpec((seq, vd), lambda b, h, i: (b, kb_ + h)),
            pl.BlockSpec((seq, vd), lambda b, h, i: (b, vb_ + h)),
            small, small, small, small,
            pl.BlockSpec((1, vd), lambda b, h, i: (0, 0)),
        ],
        out_specs=pl.BlockSpec((bq, vd), lambda b, h, i: (b * nq + i, h)),
        out_shape=jax.ShapeDtypeStruct((M, n_heads * vd), BF16),
        scratch_shapes=[pltpu.VMEM((seq, 2 * vd), BF16), pltpu.VMEM((2, bq, seq), F32)],
        compiler_params=_params("parallel", "parallel", "arbitrary"),
        name="diff_attn",
    )(proj, proj, proj, lq1, lk1, lq2, lk2, sg)


def _merge_kernel(a_ref, b_ref, wa_ref, wb_ref, *refs, n_gate_blocks):
    ga_refs, gb_refs, o_ref = refs[:n_gate_blocks], refs[n_gate_blocks:-1], refs[-1]
    ya = jnp.dot(a_ref[...], wa_ref[...], preferred_element_type=F32)
    yb = jnp.dot(b_ref[...], wb_ref[...], preferred_element_type=F32)
    gbw = ga_refs[0].shape[1]
    for t in range(n_gate_blocks):
        cols = slice(t * gbw, (t + 1) * gbw)
        o_ref[:, cols] = (ga_refs[t][...].astype(F32) * ya[:, cols]
                          + gb_refs[t][...].astype(F32) * yb[:, cols]).astype(BF16)


def _merge(a, b, wa, wb, proj, *, layer, gate_col, gbw, bm):
    M, K = a.shape
    N = wa.shape[2]
    assert gate_col % gbw == 0 and N % gbw == 0
    n_gb = N // gbw
    ga0, gb0 = gate_col // gbw, (gate_col + N) // gbw
    gate_specs = [pl.BlockSpec((bm, gbw), functools.partial(lambda i, blk: (i, blk), blk=g0 + t))
                  for g0 in (ga0, gb0) for t in range(n_gb)]
    return pl.pallas_call(
        functools.partial(_merge_kernel, n_gate_blocks=n_gb),
        grid=(M // bm,),
        in_specs=[
            pl.BlockSpec((bm, K), lambda i: (i, 0)),
            pl.BlockSpec((bm, K), lambda i: (i, 0)),
            pl.BlockSpec((None, K, N), lambda i: (layer, 0, 0)),
            pl.BlockSpec((None, K, N), lambda i: (layer, 0, 0)),
        ] + gate_specs,
        out_specs=pl.BlockSpec((bm, N), lambda i: (i, 0)),
        out_shape=jax.ShapeDtypeStruct((M, N), BF16),
        compiler_params=_params("parallel"),
        name="branch_merge",
    )(a, b, wa, wb, *([proj] * (2 * n_gb)))


def _resid_kernel(a_ref, w_ref, x_ref, gt_ref, o_ref):
    y = jnp.dot(a_ref[...], w_ref[...], preferred_element_type=F32)
    o_ref[...] = x_ref[...] + gt_ref[0] * y


def _proj_residual(a, w, x2, gt, *, layer, seq, bm, bn, in_place, name):
    M, K = a.shape
    N = w.shape[2]
    spb = seq // bm
    return pl.pallas_call(
        _resid_kernel,
        grid=(M // bm, N // bn),
        in_specs=[
            pl.BlockSpec((bm, K), lambda i, j: (i, 0)),
            pl.BlockSpec((None, K, bn), lambda i, j: (layer, 0, j)),
            pl.BlockSpec((bm, bn), lambda i, j: (i, j)),
            pl.BlockSpec((1, 1, bn), lambda i, j: (i // spb, 0, j)),
        ],
        out_specs=pl.BlockSpec((bm, bn), lambda i, j: (i, j)),
        out_shape=jax.ShapeDtypeStruct((M, N), F32),
        input_output_aliases={2: 0} if in_place else {},
        compiler_params=_params("parallel", "parallel"),
        name=name,
    )(a, w, x2, gt)


def _ffn_up_kernel(x_ref, g_ref, sc_ref, sh_ref, w1_ref, w3_ref, o_ref, h_ref):
    @pl.when(pl.program_id(1) == 0)
    def _():
        h_ref[...] = _modulated_norm(x_ref, g_ref, sc_ref, sh_ref)

    h = h_ref[...]
    a = jnp.dot(h, w1_ref[...], preferred_element_type=F32)
    b = jnp.dot(h, w3_ref[...], preferred_element_type=F32)
    o_ref[...] = (a * _sigmoid(a) * b).astype(BF16)


def _ffn_up(x2, g, sc, sh, w1, w3, *, layer, seq, bm, bn):
    M, D = x2.shape
    N = w1.shape[2]
    spb = seq // bm
    return pl.pallas_call(
        _ffn_up_kernel,
        grid=(M // bm, N // bn),
        in_specs=[
            pl.BlockSpec((bm, D), lambda i, j: (i, 0)),
            pl.BlockSpec((1, D), lambda i, j: (0, 0)),
            pl.BlockSpec((1, 1, D), lambda i, j: (i // spb, 0, 0)),
            pl.BlockSpec((1, 1, D), lambda i, j: (i // spb, 0, 0)),
            pl.BlockSpec((None, D, bn), lambda i, j: (layer, 0, j)),
            pl.BlockSpec((None, D, bn), lambda i, j: (layer, 0, j)),
        ],
        out_specs=pl.BlockSpec((bm, bn), lambda i, j: (i, j)),
        out_shape=jax.ShapeDtypeStruct((M, N), BF16),
        scratch_shapes=[pltpu.VMEM((bm, D), BF16)],
        compiler_params=_params("parallel", "arbitrary"),
        name="ffn_up",
    )(x2, g, sc, sh, w1, w3)


def _final_norm_kernel(x_ref, g_ref, o_ref):
    xf = x_ref[...]
    ms = jnp.mean(xf * xf, axis=-1, keepdims=True)
    o_ref[...] = xf * lax.rsqrt(ms + EPS) * g_ref[...]


def _final_norm(x2, g, *, bm):
    M, D = x2.shape
    return pl.pallas_call(
        _final_norm_kernel,
        grid=(M // bm,),
        in_specs=[pl.BlockSpec((bm, D), lambda i: (i, 0)), pl.BlockSpec((1, D), lambda i: (0, 0))],
        out_specs=pl.BlockSpec((bm, D), lambda i: (i, 0)),
        out_shape=jax.ShapeDtypeStruct((M, D), F32),
        compiler_params=_params("parallel"),
        name="final_norm",
    )(x2, g)


def _rope_tables(seq, dim):
    pos = jnp.arange(seq, dtype=F32)
    inv_freq = ROPE_THETA ** (-jnp.arange(0, dim, 2, dtype=F32) / dim)
    ang = pos[:, None] * inv_freq[None, :]
    cos, sin = jnp.cos(ang), jnp.sin(ang)
    reps = LANES // dim
    return (jnp.tile(jnp.concatenate([cos, cos], axis=-1), (1, reps)),
            jnp.tile(jnp.concatenate([-sin, sin], axis=-1), (1, reps)))


def kernel(x, c, ada_w, ada_b, norm1_g, norm2_g, w_in, gmlp_ln_g, gmlp_ln_b, w_s, b_s, lambda_q1, lambda_k1, lambda_q2, lambda_k2, subln_g, w_up_gmlp, w_up_attn, w_o, ffn_w1, ffn_w3, ffn_w2, final_g):
    B, S, D = x.shape
    L = w_in.shape[0]
    M = B * S
    gw = gmlp_ln_g.shape[1]
    G, chunk = w_s.shape[1], w_s.shape[2]
    dqk = lambda_q1.shape[1]
    vd = subln_g.shape[1]
    aw = w_up_attn.shape[1]
    H = aw // vd
    assert gw // G == LANES and vd == LANES and 2 * dqk == vd and chunk == LANES
    q_col, k_col, v_col, gate_col = 2 * gw, 2 * gw + aw, 2 * gw + 2 * aw, 2 * gw + 3 * aw
    assert w_in.shape[2] == gate_col + 2 * D and gw == aw

    bm = min(1024, S)
    bn = 1024
    bm_res = min(512, S)
    assert S % bm == 0 and gw == bn and D % bn == 0

    mod = _adaln(c, ada_w, ada_b).reshape(L, B, 6, 1, D)
    cos_t, sin_t = _rope_tables(S, dqk)
    x2 = x.reshape(M, D)

    w_in_b, w_s_b = w_in.astype(BF16), w_s.astype(BF16)
    w_ug_b, w_ua_b, w_o_b = w_up_gmlp.astype(BF16), w_up_attn.astype(BF16), w_o.astype(BF16)
    w1_b, w3_b, w2_b = ffn_w1.astype(BF16), ffn_w3.astype(BF16), ffn_w2.astype(BF16)

    for l in range(L):
        sh1, sc1, gt1, sh2, sc2, gt2 = [mod[l, :, t] for t in range(6)]
        lambda_init = 0.8 - 0.6 * math.exp(-0.3 * l)

        proj = _inproj(x2, norm1_g[l].reshape(1, D), sc1, sh1, w_in_b,
                       gmlp_ln_g[l].reshape(1, gw), gmlp_ln_b[l].reshape(1, gw), cos_t, sin_t,
                       layer=l, seq=S, n_heads=H, qk_scale=dqk ** -0.5 * math.log2(math.e),
                       bm=bm, bn=bn)
        bs = jnp.broadcast_to(b_s[l][:, :, None], (G, chunk, LANES))
        branch_a = _gmlp(proj, w_s_b, bs, layer=l, width=gw, chunk=chunk, bm=min(512, S))
        branch_b = _attention(proj, lambda_q1[l].reshape(1, dqk), lambda_k1[l].reshape(1, dqk),
                              lambda_q2[l].reshape(1, dqk), lambda_k2[l].reshape(1, dqk),
                              subln_g[l].reshape(1, vd), batch=B, seq=S, n_heads=H, vd=vd,
                              q_col=q_col, k_col=k_col, v_col=v_col, lambda_init=lambda_init,
                              bq=min(1024, S), ck=min(512, S))
        merged = _merge(branch_a, branch_b, w_ug_b, w_ua_b, proj, layer=l, gate_col=gate_col,
                        gbw=bn, bm=bm_res)
        x2 = _proj_residual(merged, w_o_b, x2, gt1, layer=l, seq=S, bm=bm_res, bn=D,
                            in_place=l > 0, name="wo_residual")

        t = _ffn_up(x2, norm2_g[l].reshape(1, D), sc2, sh2, w1_b, w3_b, layer=l, seq=S,
                    bm=bm, bn=512)
        x2 = _proj_residual(t, w2_b, x2, gt2, layer=l, seq=S, bm=bm, bn=512, in_place=True,
                            name="ffn_down_residual")

    return _final_norm(x2, final_g.reshape(1, D), bm=min(512, S)).reshape(B, S, D)
```

```python
import functools
import math

import jax
import jax.numpy as jnp
from jax import lax
from jax.experimental import pallas as pl
from jax.experimental.pallas import tpu as pltpu

BF16 = jnp.bfloat16
F32 = jnp.float32

EPS = 1e-6
ROPE_THETA = 10000.0
LANES = 128
V7X_VMEM_LIMIT = 56 * 1024 * 1024


def _params(*sem):
    return pltpu.CompilerParams(dimension_semantics=sem, vmem_limit_bytes=V7X_VMEM_LIMIT)


def _sigmoid(x):
    return 1.0 / (1.0 + jnp.exp(-x))


def _adaln_kernel(c_ref, w_ref, b_ref, o_ref):
    c = c_ref[...]
    c_act = (c * _sigmoid(c)).astype(BF16)
    acc = jnp.dot(c_act, w_ref[...].astype(BF16), preferred_element_type=F32)
    o_ref[...] = acc + b_ref[...]


def _adaln(c, ada_w, ada_b, bn=1024):
    L, D, N = ada_w.shape
    B = c.shape[0]
    bn = min(bn, N)
    return pl.pallas_call(
        _adaln_kernel,
        grid=(L, N // bn),
        in_specs=[
            pl.BlockSpec((B, D), lambda l, j: (0, 0)),
            pl.BlockSpec((None, D, bn), lambda l, j: (l, 0, j)),
            pl.BlockSpec((None, 1, bn), lambda l, j: (l, 0, j)),
        ],
        out_specs=pl.BlockSpec((None, B, bn), lambda l, j: (l, 0, j)),
        out_shape=jax.ShapeDtypeStruct((L, B, N), F32),
        compiler_params=_params("parallel", "parallel"),
        name="adaln_mod",
    )(c, ada_w, ada_b.reshape(L, 1, N))


def _modulated_norm(x_ref, g_ref, sc_ref, sh_ref):
    xf = x_ref[...]
    ms = jnp.mean(xf * xf, axis=-1, keepdims=True)
    y = xf * lax.rsqrt(ms + EPS) * g_ref[...]
    return (y * (1.0 + sc_ref[0]) + sh_ref[0]).astype(BF16)


def _inproj_kernel(x_ref, g_ref, sc_ref, sh_ref, w_ref, lng_ref, lnb_ref, cos_ref, sin_ref,
                   o_ref, h_ref, *, n_heads, qk_scale):
    j = pl.program_id(1)
    bm = o_ref.shape[0]

    @pl.when(j == 0)
    def _():
        h_ref[...] = _modulated_norm(x_ref, g_ref, sc_ref, sh_ref)

    def proj():
        return jnp.dot(h_ref[...], w_ref[...], preferred_element_type=F32)

    @pl.when(j == 0)
    def _():
        o_ref[...] = jax.nn.gelu(proj()).astype(BF16)

    @pl.when(j == 1)
    def _():
        v = jax.nn.gelu(proj())
        mu = jnp.mean(v, axis=-1, keepdims=True)
        vc = v - mu
        var = jnp.mean(vc * vc, axis=-1, keepdims=True)
        o_ref[...] = (vc * lax.rsqrt(var + EPS) * lng_ref[...] + lnb_ref[...]).astype(BF16)

    def rope(scale):
        acc = proj()
        cos = cos_ref[...]
        sin = sin_ref[...]
        lane = lax.broadcasted_iota(jnp.int32, (bm, LANES), 1)
        first_half = (lane & (LANES // 4)) == 0
        for h in range(n_heads):
            xh = acc[:, h * LANES:(h + 1) * LANES]
            partner = jnp.where(first_half, pltpu.roll(xh, 3 * LANES // 4, 1),
                                pltpu.roll(xh, LANES // 4, 1))
            r = xh * cos + partner * sin
            if scale != 1.0:
                r = r * scale
            o_ref[:, h * LANES:(h + 1) * LANES] = r.astype(BF16)

    @pl.when(j == 2)
    def _():
        rope(qk_scale)

    @pl.when(j == 3)
    def _():
        rope(1.0)

    @pl.when(j == 4)
    def _():
        o_ref[...] = proj().astype(BF16)

    @pl.when(j >= 5)
    def _():
        o_ref[...] = _sigmoid(proj()).astype(BF16)


def _inproj(x2, g, sc, sh, w, lng, lnb, cos_t, sin_t, *, layer, seq, n_heads, qk_scale, bm, bn):
    M, D = x2.shape
    N = w.shape[2]
    spb = seq // bm
    kern = functools.partial(_inproj_kernel, n_heads=n_heads, qk_scale=qk_scale)
    return pl.pallas_call(
        kern,
        grid=(M // bm, N // bn),
        in_specs=[
            pl.BlockSpec((bm, D), lambda i, j: (i, 0)),
            pl.BlockSpec((1, D), lambda i, j: (0, 0)),
            pl.BlockSpec((1, 1, D), lambda i, j: (i // spb, 0, 0)),
            pl.BlockSpec((1, 1, D), lambda i, j: (i // spb, 0, 0)),
            pl.BlockSpec((None, D, bn), lambda i, j: (layer, 0, j)),
            pl.BlockSpec((1, bn), lambda i, j: (0, 0)),
            pl.BlockSpec((1, bn), lambda i, j: (0, 0)),
            pl.BlockSpec((bm, LANES), lambda i, j: (i % spb, 0)),
            pl.BlockSpec((bm, LANES), lambda i, j: (i % spb, 0)),
        ],
        out_specs=pl.BlockSpec((bm, bn), lambda i, j: (i, j)),
        out_shape=jax.ShapeDtypeStruct((M, N), BF16),
        scratch_shapes=[pltpu.VMEM((bm, D), BF16)],
        compiler_params=_params("parallel", "arbitrary"),
        name="inproj",
    )(x2, g, sc, sh, w, lng, lnb, cos_t, sin_t)


def _gmlp_kernel(u_ref, v_ref, ws_ref, bs_ref, o_ref, *, chunk):
    bm, width = o_ref.shape
    for c in range(bm // chunk):
        rows = slice(c * chunk, (c + 1) * chunk)
        for g in range(width // LANES):
            cols = slice(g * LANES, (g + 1) * LANES)
            mixed = jnp.dot(ws_ref[g], v_ref[rows, cols], preferred_element_type=F32) + bs_ref[g]
            o_ref[rows, cols] = (u_ref[rows, cols].astype(F32) * mixed).astype(BF16)


def _gmlp(proj, ws, bs, *, layer, width, chunk, bm):
    M = proj.shape[0]
    G = ws.shape[1]
    return pl.pallas_call(
        functools.partial(_gmlp_kernel, chunk=chunk),
        grid=(M // bm,),
        in_specs=[
            pl.BlockSpec((bm, width), lambda i: (i, 0)),
            pl.BlockSpec((bm, width), lambda i: (i, 1)),
            pl.BlockSpec((None, G, chunk, chunk), lambda i: (layer, 0, 0, 0)),
            pl.BlockSpec((G, chunk, LANES), lambda i: (0, 0, 0)),
        ],
        out_specs=pl.BlockSpec((bm, width), lambda i: (i, 0)),
        out_shape=jax.ShapeDtypeStruct((M, width), BF16),
        compiler_params=_params("parallel"),
        name="gmlp_gate",
    )(proj, proj, ws, bs)


def _attn_kernel(q_ref, k_ref, v_ref, lq1_ref, lk1_ref, lq2_ref, lk2_ref, sg_ref, o_ref,
                 vext_ref, s_ref, *, lambda_init, ck):
    bq, vd = q_ref.shape
    dqk = vd // 2

    @pl.when(pl.program_id(2) == 0)
    def _():
        vext_ref[:, :vd] = v_ref[...]
        vext_ref[:, vd:] = jnp.ones_like(v_ref)

    q = q_ref[...]
    lane = lax.broadcasted_iota(jnp.int32, (bq, vd), 1)
    zero = jnp.zeros_like(q)
    q_maps = (jnp.where(lane < dqk, q, zero),
              jnp.where(lane >= dqk, q, zero))
    nt = (((1,), (1,)), ((), ()))
    n_chunks = k_ref.shape[0] // ck

    part_max = [jnp.full((bq, LANES), -jnp.inf, F32) for _ in q_maps]
    for c in range(n_chunks):
        kc = k_ref[c * ck:(c + 1) * ck, :]
        for mi, qm in enumerate(q_maps):
            s = lax.dot_general(qm, kc, nt, preferred_element_type=F32)
            s_ref[mi, :, c * ck:(c + 1) * ck] = s
            for t in range(ck // LANES):
                part_max[mi] = jnp.maximum(part_max[mi], s[:, t * LANES:(t + 1) * LANES])
    row_max = [jnp.max(pm, axis=-1, keepdims=True) for pm in part_max]

    pv = [jnp.zeros((bq, 2 * vd), F32) for _ in q_maps]
    for c in range(n_chunks):
        vc = vext_ref[c * ck:(c + 1) * ck, :]
        for mi in range(len(q_maps)):
            p = jnp.exp2((s_ref[mi, :, c * ck:(c + 1) * ck] - row_max[mi]).astype(BF16))
            pv[mi] = pv[mi] + jnp.dot(p, vc, preferred_element_type=F32)
    o1 = pv[0][:, :vd] / pv[0][:, vd:]
    o2 = pv[1][:, :vd] / pv[1][:, vd:]

    lam = (jnp.exp(jnp.sum(lq1_ref[...] * lk1_ref[...], axis=-1, keepdims=True))
           - jnp.exp(jnp.sum(lq2_ref[...] * lk2_ref[...], axis=-1, keepdims=True))
           + lambda_init)
    o = o1 - lam * o2
    ms = jnp.mean(o * o, axis=-1, keepdims=True)
    o = o * lax.rsqrt(ms + EPS) * sg_ref[...] * (1.0 - lambda_init)
    o_ref[...] = o.astype(BF16)


def _attention(proj, lq1, lk1, lq2, lk2, sg, *, batch, seq, n_heads, vd, q_col, k_col, v_col,
               lambda_init, bq, ck):
    M = proj.shape[0]
    nq = seq // bq
    dqk = lq1.shape[-1]
    qb, kb_, vb_ = q_col // vd, k_col // vd, v_col // vd
    small = pl.BlockSpec((1, dqk), lambda b, h, i: (0, 0))
    return pl.pallas_call(
        functools.partial(_attn_kernel, lambda_init=lambda_init, ck=ck),
        grid=(batch, n_heads, nq),
        in_specs=[
            pl.BlockSpec((bq, vd), lambda b, h, i: (b * nq + i, qb + h)),
            pl.BlockSpec((seq, vd), lambda b, h, i: (b, kb_ + h)),
            pl.BlockSpec((seq, vd), lambda b, h, i: (b, vb_ + h)),
            small, small, small, small,
            pl.BlockSpec((1, vd), lambda b, h, i: (0, 0)),
        ],
        out_specs=pl.BlockSpec((bq, vd), lambda b, h, i: (b * nq + i, h)),
        out_shape=jax.ShapeDtypeStruct((M, n_heads * vd), BF16),
        scratch_shapes=[pltpu.VMEM((seq, 2 * vd), BF16), pltpu.VMEM((2, bq, seq), F32)],
        compiler_params=_params("parallel", "parallel", "arbitrary"),
        name="diff_attn",
    )(proj, proj, proj, lq1, lk1, lq2, lk2, sg)


def _merge_kernel(a_ref, b_ref, wa_ref, wb_ref, *refs, n_gate_blocks):
    ga_refs, gb_refs, o_ref = refs[:n_gate_blocks], refs[n_gate_blocks:-1], refs[-1]
    ya = jnp.dot(a_ref[...], wa_ref[...], preferred_element_type=F32)
    yb = jnp.dot(b_ref[...], wb_ref[...], preferred_element_type=F32)
    gbw = ga_refs[0].shape[1]
    for t in range(n_gate_blocks):
        cols = slice(t * gbw, (t + 1) * gbw)
        o_ref[:, cols] = (ga_refs[t][...].astype(F32) * ya[:, cols]
                          + gb_refs[t][...].astype(F32) * yb[:, cols]).astype(BF16)


def _merge(a, b, wa, wb, proj, *, layer, gate_col, gbw, bm):
    M, K = a.shape
    N = wa.shape[2]
    assert gate_col % gbw == 0 and N % gbw == 0
    n_gb = N // gbw
    ga0, gb0 = gate_col // gbw, (gate_col + N) // gbw
    gate_specs = [pl.BlockSpec((bm, gbw), functools.partial(lambda i, blk: (i, blk), blk=g0 + t))
                  for g0 in (ga0, gb0) for t in range(n_gb)]
    return pl.pallas_call(
        functools.partial(_merge_kernel, n_gate_blocks=n_gb),
        grid=(M // bm,),
        in_specs=[
            pl.BlockSpec((bm, K), lambda i: (i, 0)),
            pl.BlockSpec((bm, K), lambda i: (i, 0)),
            pl.BlockSpec((None, K, N), lambda i: (layer, 0, 0)),
            pl.BlockSpec((None, K, N), lambda i: (layer, 0, 0)),
        ] + gate_specs,
        out_specs=pl.BlockSpec((bm, N), lambda i: (i, 0)),
        out_shape=jax.ShapeDtypeStruct((M, N), BF16),
        compiler_params=_params("parallel"),
        name="branch_merge",
    )(a, b, wa, wb, *([proj] * (2 * n_gb)))


def _merge_wo_kernel(a_ref, b_ref, wa_ref, wb_ref, wo_ref, x_ref, gt_ref, *refs, n_gate_blocks):
    ga_refs, gb_refs = refs[:n_gate_blocks], refs[n_gate_blocks:2 * n_gate_blocks]
    o_ref, m_ref = refs[2 * n_gate_blocks:]
    ya = jnp.dot(a_ref[...], wa_ref[...], preferred_element_type=F32)
    yb = jnp.dot(b_ref[...], wb_ref[...], preferred_element_type=F32)
    gbw = ga_refs[0].shape[1]
    for t in range(n_gate_blocks):
        cols = slice(t * gbw, (t + 1) * gbw)
        m_ref[:, cols] = (ga_refs[t][...].astype(F32) * ya[:, cols]
                          + gb_refs[t][...].astype(F32) * yb[:, cols]).astype(BF16)
    y = jnp.dot(m_ref[...], wo_ref[...], preferred_element_type=F32)
    o_ref[...] = x_ref[...] + gt_ref[0] * y


def _merge_wo(a, b, wa, wb, wo, x2, gt, proj, *, layer, seq, gate_col, gbw, bm, in_place):
    M, K = a.shape
    N, D = wa.shape[2], wo.shape[2]
    assert gate_col % gbw == 0 and N % gbw == 0
    n_gb = N // gbw
    spb = seq // bm
    ga0, gb0 = gate_col // gbw, (gate_col + N) // gbw
    gate_specs = [pl.BlockSpec((bm, gbw), functools.partial(lambda i, blk: (i, blk), blk=g0 + t))
                  for g0 in (ga0, gb0) for t in range(n_gb)]
    return pl.pallas_call(
        functools.partial(_merge_wo_kernel, n_gate_blocks=n_gb),
        grid=(M // bm,),
        in_specs=[
            pl.BlockSpec((bm, K), lambda i: (i, 0)),
            pl.BlockSpec((bm, K), lambda i: (i, 0)),
            pl.BlockSpec((None, K, N), lambda i: (layer, 0, 0)),
            pl.BlockSpec((None, K, N), lambda i: (layer, 0, 0)),
            pl.BlockSpec((None, N, D), lambda i: (layer, 0, 0)),
            pl.BlockSpec((bm, D), lambda i: (i, 0)),
            pl.BlockSpec((1, 1, D), lambda i: (i // spb, 0, 0)),
        ] + gate_specs,
        out_specs=pl.BlockSpec((bm, D), lambda i: (i, 0)),
        out_shape=jax.ShapeDtypeStruct((M, D), F32),
        scratch_shapes=[pltpu.VMEM((bm, N), BF16)],
        input_output_aliases={5: 0} if in_place else {},
        compiler_params=_params("parallel"),
        name="merge_wo_residual",
    )(a, b, wa, wb, wo, x2, gt, *([proj] * (2 * n_gb)))


def _resid_kernel(a_ref, w_ref, x_ref, gt_ref, o_ref):
    y = jnp.dot(a_ref[...], w_ref[...], preferred_element_type=F32)
    o_ref[...] = x_ref[...] + gt_ref[0] * y


def _proj_residual(a, w, x2, gt, *, layer, seq, bm, bn, in_place, name):
    M, K = a.shape
    N = w.shape[2]
    spb = seq // bm
    return pl.pallas_call(
        _resid_kernel,
        grid=(M // bm, N // bn),
        in_specs=[
            pl.BlockSpec((bm, K), lambda i, j: (i, 0)),
            pl.BlockSpec((None, K, bn), lambda i, j: (layer, 0, j)),
            pl.BlockSpec((bm, bn), lambda i, j: (i, j)),
            pl.BlockSpec((1, 1, bn), lambda i, j: (i // spb, 0, j)),
        ],
        out_specs=pl.BlockSpec((bm, bn), lambda i, j: (i, j)),
        out_shape=jax.ShapeDtypeStruct((M, N), F32),
        input_output_aliases={2: 0} if in_place else {},
        compiler_params=_params("parallel", "parallel"),
        name=name,
    )(a, w, x2, gt)


def _ffn_up_kernel(x_ref, g_ref, sc_ref, sh_ref, w1_ref, w3_ref, o_ref, h_ref):
    @pl.when(pl.program_id(1) == 0)
    def _():
        h_ref[...] = _modulated_norm(x_ref, g_ref, sc_ref, sh_ref)

    h = h_ref[...]
    a = jnp.dot(h, w1_ref[...], preferred_element_type=F32)
    b = jnp.dot(h, w3_ref[...], preferred_element_type=F32)
    o_ref[...] = (a * _sigmoid(a) * b).astype(BF16)


def _ffn_up(x2, g, sc, sh, w1, w3, *, layer, seq, bm, bn):
    M, D = x2.shape
    N = w1.shape[2]
    spb = seq // bm
    return pl.pallas_call(
        _ffn_up_kernel,
        grid=(M // bm, N // bn),
        in_specs=[
            pl.BlockSpec((bm, D), lambda i, j: (i, 0)),
            pl.BlockSpec((1, D), lambda i, j: (0, 0)),
            pl.BlockSpec((1, 1, D), lambda i, j: (i // spb, 0, 0)),
            pl.BlockSpec((1, 1, D), lambda i, j: (i // spb, 0, 0)),
            pl.BlockSpec((None, D, bn), lambda i, j: (layer, 0, j)),
            pl.BlockSpec((None, D, bn), lambda i, j: (layer, 0, j)),
        ],
        out_specs=pl.BlockSpec((bm, bn), lambda i, j: (i, j)),
        out_shape=jax.ShapeDtypeStruct((M, N), BF16),
        scratch_shapes=[pltpu.VMEM((bm, D), BF16)],
        compiler_params=_params("parallel", "arbitrary"),
        name="ffn_up",
    )(x2, g, sc, sh, w1, w3)


def _final_norm_kernel(x_ref, g_ref, o_ref):
    xf = x_ref[...]
    ms = jnp.mean(xf * xf, axis=-1, keepdims=True)
    o_ref[...] = xf * lax.rsqrt(ms + EPS) * g_ref[...]


def _final_norm(x2, g, *, bm):
    M, D = x2.shape
    return pl.pallas_call(
        _final_norm_kernel,
        grid=(M // bm,),
        in_specs=[pl.BlockSpec((bm, D), lambda i: (i, 0)), pl.BlockSpec((1, D), lambda i: (0, 0))],
        out_specs=pl.BlockSpec((bm, D), lambda i: (i, 0)),
        out_shape=jax.ShapeDtypeStruct((M, D), F32),
        compiler_params=_params("parallel"),
        name="final_norm",
    )(x2, g)


def _rope_tables(seq, dim):
    pos = jnp.arange(seq, dtype=F32)
    inv_freq = ROPE_THETA ** (-jnp.arange(0, dim, 2, dtype=F32) / dim)
    ang = pos[:, None] * inv_freq[None, :]
    cos, sin = jnp.cos(ang), jnp.sin(ang)
    reps = LANES // dim
    return (jnp.tile(jnp.concatenate([cos, cos], axis=-1), (1, reps)),
            jnp.tile(jnp.concatenate([-sin, sin], axis=-1), (1, reps)))


def kernel(x, c, ada_w, ada_b, norm1_g, norm2_g, w_in, gmlp_ln_g, gmlp_ln_b, w_s, b_s, lambda_q1, lambda_k1, lambda_q2, lambda_k2, subln_g, w_up_gmlp, w_up_attn, w_o, ffn_w1, ffn_w3, ffn_w2, final_g):
    B, S, D = x.shape
    L = w_in.shape[0]
    M = B * S
    gw = gmlp_ln_g.shape[1]
    G, chunk = w_s.shape[1], w_s.shape[2]
    dqk = lambda_q1.shape[1]
    vd = subln_g.shape[1]
    aw = w_up_attn.shape[1]
    H = aw // vd
    assert gw // G == LANES and vd == LANES and 2 * dqk == vd and chunk == LANES
    q_col, k_col, v_col, gate_col = 2 * gw, 2 * gw + aw, 2 * gw + 2 * aw, 2 * gw + 3 * aw
    assert w_in.shape[2] == gate_col + 2 * D and gw == aw

    bm = min(1024, S)
    bn = 1024
    bm_res = min(512, S)
    assert S % bm == 0 and gw == bn and D % bn == 0

    mod = _adaln(c, ada_w, ada_b).reshape(L, B, 6, 1, D)
    cos_t, sin_t = _rope_tables(S, dqk)
    x2 = x.reshape(M, D)

    w_in_b, w_s_b = w_in.astype(BF16), w_s.astype(BF16)
    w_ug_b, w_ua_b, w_o_b = w_up_gmlp.astype(BF16), w_up_attn.astype(BF16), w_o.astype(BF16)
    w1_b, w3_b, w2_b = ffn_w1.astype(BF16), ffn_w3.astype(BF16), ffn_w2.astype(BF16)

    for l in range(L):
        sh1, sc1, gt1, sh2, sc2, gt2 = [mod[l, :, t] for t in range(6)]
        lambda_init = 0.8 - 0.6 * math.exp(-0.3 * l)

        proj = _inproj(x2, norm1_g[l].reshape(1, D), sc1, sh1, w_in_b,
                       gmlp_ln_g[l].reshape(1, gw), gmlp_ln_b[l].reshape(1, gw), cos_t, sin_t,
                       layer=l, seq=S, n_heads=H, qk_scale=dqk ** -0.5 * math.log2(math.e),
                       bm=bm, bn=bn)
        bs = jnp.broadcast_to(b_s[l][:, :, None], (G, chunk, LANES))
        branch_a = _gmlp(proj, w_s_b, bs, layer=l, width=gw, chunk=chunk, bm=min(512, S))
        branch_b = _attention(proj, lambda_q1[l].reshape(1, dqk), lambda_k1[l].reshape(1, dqk),
                              lambda_q2[l].reshape(1, dqk), lambda_k2[l].reshape(1, dqk),
                              subln_g[l].reshape(1, vd), batch=B, seq=S, n_heads=H, vd=vd,
                              q_col=q_col, k_col=k_col, v_col=v_col, lambda_init=lambda_init,
                              bq=min(1024, S), ck=min(512, S))
        x2 = _merge_wo(branch_a, branch_b, w_ug_b, w_ua_b, w_o_b, x2, gt1, proj, layer=l, seq=S,
                       gate_col=gate_col, gbw=bn, bm=min(256, S), in_place=l > 0)

        t = _ffn_up(x2, norm2_g[l].reshape(1, D), sc2, sh2, w1_b, w3_b, layer=l, seq=S,
                    bm=bm, bn=512)
        x2 = _proj_residual(t, w2_b, x2, gt2, layer=l, seq=S, bm=bm, bn=512, in_place=True,
                            name="ffn_down_residual")

    return _final_norm(x2, final_g.reshape(1, D), bm=min(512, S)).reshape(B, S, D)
```
